```python
import math
import jax
import jax.numpy as jnp
from jax import lax
import numpy as np

D_MODEL = 2048
BATCH = 4
SEQ = 2048
DEPTH = 1
DEC_BATCH = 32
DEC_SEQ = 1
PAST_LEN = 16384
PAGE_SIZE = 128

A_HEAD_DIM = 128
A_HEADS = D_MODEL // 256
A_WIDTH = A_HEADS * A_HEAD_DIM
MOBA_BLOCK = 256
MOBA_TOPK = 3
QUERY_BLOCK = 128
POOL_WINDOWS = (2, 4, 8, 16)
POOL_GROUPS = len(POOL_WINDOWS)
POOL_WIDTH = D_MODEL // 2
POOL_GC = POOL_WIDTH // POOL_GROUPS
POOL_CTX = max(POOL_WINDOWS) - 1
N_MEM = 256
M_HEADS = 4
M_HEAD_DIM = D_MODEL // 8
M_WIDTH = M_HEADS * M_HEAD_DIM
N_BRANCH = 3
D_FF = 4 * D_MODEL
DN_ALPHA = float((2 * DEPTH) ** 0.25)
DN_BETA = float((8 * DEPTH) ** -0.25)
LN_EPS = 1e-5
NEG = -1e30
Q_OFF = 0
K_OFF = A_WIDTH
V_OFF = 2 * A_WIDTH
U_OFF = 3 * A_WIDTH
MQ_OFF = U_OFF + POOL_WIDTH
G_OFF = MQ_OFF + M_WIDTH
IN_COLS = G_OFF + N_BRANCH * D_MODEL

kernel_name = "moba_pool_memxattn_gated_deepnorm_decode"


def layer_norm(x, g, b):
    xf = x.astype(jnp.float32)
    mu = jnp.mean(xf, axis=-1, keepdims=True)
    var = jnp.mean(jnp.square(xf - mu), axis=-1, keepdims=True)
    return ((xf - mu) * lax.rsqrt(var + LN_EPS) * g + b).astype(x.dtype)


def split_in(p):
    B, T = p.shape[:2]
    q = p[..., Q_OFF:K_OFF].reshape(B, T, A_HEADS, A_HEAD_DIM)
    k = p[..., K_OFF:V_OFF].reshape(B, T, A_HEADS, A_HEAD_DIM)
    v = p[..., V_OFF:U_OFF].reshape(B, T, A_HEADS, A_HEAD_DIM)
    u = p[..., U_OFF:MQ_OFF]
    mq = p[..., MQ_OFF:G_OFF].reshape(B, T, M_HEADS, M_HEAD_DIM)
    gates = p[..., G_OFF:IN_COLS].reshape(B, T, N_BRANCH, D_MODEL)
    return q, k, v, u, mq, gates


def moba_select(q, means, qpos):
    nb = means.shape[1]
    own = qpos // MOBA_BLOCK
    g = jnp.einsum('bthd,bnhd->bthn', q.astype(jnp.float32), means)
    past = (jnp.arange(nb)[None, :] < own[:, None])[None, :, None, :]
    g = jnp.where(past, g, NEG)
    _, idx = lax.top_k(g, min(MOBA_TOPK, nb))
    valid = idx < own[None, :, None, None]
    return idx, valid


def moba_combine(q, k_sel, v_sel, sel_valid, k_own, v_own, own_mask, own_spec):
    scale = q.shape[-1] ** -0.5
    s_sel = jnp.einsum('bthd,bthjkd->bthjk', q, k_sel).astype(jnp.float32) * scale
    s_sel = jnp.where(sel_valid[..., None], s_sel, NEG)
    s_own = jnp.einsum('bthd,' + own_spec + '->bthk', q, k_own).astype(jnp.float32) * scale
    s_own = jnp.where(own_mask, s_own, NEG)
    B, T, H, J, K = s_sel.shape
    p = jax.nn.softmax(jnp.concatenate([s_sel.reshape(B, T, H, J * K), s_own], axis=-1), axis=-1)
    p_sel = p[..., :J * K].reshape(B, T, H, J, K).astype(v_sel.dtype)
    p_own = p[..., J * K:].astype(v_own.dtype)
    return (jnp.einsum('bthjk,bthjkd->bthd', p_sel, v_sel)
            + jnp.einsum('bthk,' + own_spec + '->bthd', p_own, v_own))


def moba_prompt(q, k, v):
    B, S, H, hd = q.shape
    nb = -(-S // MOBA_BLOCK)
    pad = ((0, 0), (0, nb * MOBA_BLOCK - S), (0, 0), (0, 0))
    kb = jnp.pad(k, pad).reshape(B, nb, MOBA_BLOCK, H, hd)
    vb = jnp.pad(v, pad).reshape(B, nb, MOBA_BLOCK, H, hd)
    means = jnp.mean(kb.astype(jnp.float32), axis=2)
    idx, valid = moba_select(q, means, jnp.arange(S))
    kbh = kb.transpose(0, 3, 1, 2, 4)
    vbh = vb.transpose(0, 3, 1, 2, 4)
    bi = jnp.arange(B)[:, None, None, None]
    hi = jnp.arange(H)[None, None, :, None]
    r = jnp.arange(MOBA_BLOCK)
    qr = jnp.arange(QUERY_BLOCK)

    def chunk(c):
        t0 = c * QUERY_BLOCK
        qc = lax.dynamic_slice_in_dim(q, t0, QUERY_BLOCK, axis=1)
        ic = lax.dynamic_slice_in_dim(idx, t0, QUERY_BLOCK, axis=1)
        vc = lax.dynamic_slice_in_dim(valid, t0, QUERY_BLOCK, axis=1)
        ob = t0 // MOBA_BLOCK
        k_sel = kbh[bi, hi, ic]
        v_sel = vbh[bi, hi, ic]
        k_own = lax.dynamic_index_in_dim(kb, ob, axis=1, keepdims=False)
        v_own = lax.dynamic_index_in_dim(vb, ob, axis=1, keepdims=False)
        own_mask = ((ob * MOBA_BLOCK + r)[None, :] <= (t0 + qr)[:, None])[None, :, None, :]
        return moba_combine(qc, k_sel, v_sel, vc, k_own, v_own, own_mask, 'bkhd')

    out = lax.map(chunk, jnp.arange(S // QUERY_BLOCK))
    return out.transpose(1, 0, 2, 3, 4).reshape(B, S, H, hd)


def fetch_rows(cache, layer, new, page_table, pos, b_idx, h_idx, past):
    n_pages = page_table.shape[1]
    page = cache.shape[2]
    T = new.shape[1]
    phys = page_table[b_idx, jnp.minimum(pos // page, n_pages - 1)]
    from_cache = cache[layer, phys, pos % page, h_idx]
    from_new = new[b_idx, jnp.clip(pos - past, 0, T - 1), h_idx]
    return jnp.where((pos < past)[..., None], from_cache, from_new)


def moba_sample(q, k_new, v_new, cache_k, cache_v, page_table, layer):
    B, T, H, hd = q.shape
    n_pages = page_table.shape[1]
    page = cache_k.shape[2]
    past = n_pages * page
    nb = -(-(past + T) // MOBA_BLOCK)
    qpos = past + jnp.arange(T)
    page_sum = jnp.sum(cache_k[layer, page_table], axis=2, dtype=jnp.float32)
    blk = jnp.arange(nb)[None, :]
    oh_p = (((jnp.arange(n_pages) * page) // MOBA_BLOCK)[:, None] == blk).astype(jnp.float32)
    oh_n = ((qpos // MOBA_BLOCK)[:, None] == blk).astype(jnp.float32)
    means = (jnp.einsum('bphd,pn->bnhd', page_sum, oh_p)
             + jnp.einsum('bthd,tn->bnhd', k_new.astype(jnp.float32), oh_n)) / MOBA_BLOCK
    idx, valid = moba_select(q, means, qpos)
    r = jnp.arange(MOBA_BLOCK)
    pos_sel = idx[..., None] * MOBA_BLOCK + r
    bs = jnp.arange(B)[:, None, None, None, None]
    hs = jnp.arange(H)[None, None, :, None, None]
    k_sel = fetch_rows(cache_k, layer, k_new, page_table, pos_sel, bs, hs, past)
    v_sel = fetch_rows(cache_v, layer, v_new, page_table, pos_sel, bs, hs, past)
    pos_own = (((qpos // MOBA_BLOCK) * MOBA_BLOCK)[:, None] + r)[None, :, None, :]
    bo = jnp.arange(B)[:, None, None, None]
    ho = jnp.arange(H)[None, None, :, None]
    k_own = fetch_rows(cache_k, layer, k_new, page_table, pos_own, bo, ho, past)
    v_own = fetch_rows(cache_v, layer, v_new, page_table, pos_own, bo, ho, past)
    own_mask = pos_own <= qpos[None, :, None, None]
    return moba_combine(q, k_sel, v_sel, valid, k_own, v_own, own_mask, 'bthkd')


def pool_mix(u, prefix, pos0):
    B, T, C = u.shape
    P = prefix.shape[1]
    ext = jnp.concatenate([prefix.astype(u.dtype), u], axis=1)
    cs = jnp.cumsum(ext.astype(jnp.float32), axis=1)
    cs = jnp.concatenate([jnp.zeros((B, 1, C), jnp.float32), cs], axis=1)
    pos = pos0 + jnp.arange(T)
    outs = []
    for g, w in enumerate(POOL_WINDOWS):
        sl = slice(g * POOL_GC, (g + 1) * POOL_GC)
        win = cs[:, P + 1:P + 1 + T, sl] - cs[:, P + 1 - w:P + 1 - w + T, sl]
        cnt = jnp.minimum(w, pos + 1).astype(jnp.float32)
        outs.append(win / cnt[None, :, None] - u[:, :, sl].astype(jnp.float32))
    return jnp.concatenate(outs, axis=-1).astype(u.dtype), ext[:, -P:]


def pool_branch(u, prefix, pos0, w_pool_l, scale_l):
    B, T, _ = u.shape
    m, tail = pool_mix(u, prefix, pos0)
    m = jnp.einsum('btgc,gcd->btgd', m.reshape(B, T, POOL_GROUPS, POOL_GC), w_pool_l)
    return m.reshape(B, T, POOL_WIDTH) * scale_l, tail


def mem_attend(q, k, v):
    s = jnp.einsum('bthd,bmhd->bthm', q, k).astype(jnp.float32) * (q.shape[-1] ** -0.5)
    p = jax.nn.softmax(s, axis=-1).astype(v.dtype)
    return jnp.einsum('bthm,bmhd->bthd', p, v)


def mem_kv(mem, w):
    B = mem.shape[0]
    kv = mem @ w
    k = kv[..., :M_WIDTH].reshape(B, N_MEM, M_HEADS, M_HEAD_DIM)
    v = kv[..., M_WIDTH:].reshape(B, N_MEM, M_HEADS, M_HEAD_DIM)
    return k, v


def merge_branches(attn, pool_out, mem_out, gate_logits, w_ba, w_bp, w_bm, w_o):
    B, T = attn.shape[:2]
    ba = attn.reshape(B, T, A_WIDTH) @ w_ba
    bp = pool_out @ w_bp
    bm = mem_out.reshape(B, T, M_WIDTH) @ w_bm
    g = jax.nn.sigmoid(gate_logits.astype(jnp.float32)).astype(attn.dtype)
    return (g[:, :, 0] * ba + g[:, :, 1] * bp + g[:, :, 2] * bm) @ w_o


def post_norm_block(x, mix, ln1_g, ln1_b, w_up, w_down, ln2_g, ln2_b):
    h = layer_norm(DN_ALPHA * x + mix, ln1_g, ln1_b)
    f = jnp.square(jax.nn.relu(h @ w_up)) @ w_down
    return layer_norm(DN_ALPHA * h + f, ln2_g, ln2_b)


def setup_inputs(seed: int = 0) -> dict:
    key = jax.random.key(seed)
    ks = jax.random.split(key, 24)
    f32 = jnp.float32
    n_pages = PAST_LEN // PAGE_SIZE
    n_used = DEC_BATCH * n_pages
    n_phys = n_used + n_used // 4
    nrm = lambda k, shape, s=1.0: jax.random.normal(k, shape, f32) * s
    page_table = jax.random.permutation(ks[0], n_phys)[:n_used].reshape(DEC_BATCH, n_pages).astype(jnp.int32)
    return {
        "x_prompt": nrm(ks[1], (BATCH, SEQ, D_MODEL)),
        "x_sample": nrm(ks[2], (DEC_BATCH, DEC_SEQ, D_MODEL)),
        "cache_k": nrm(ks[3], (DEPTH, n_phys, PAGE_SIZE, A_HEADS, A_HEAD_DIM)),
        "cache_v": nrm(ks[4], (DEPTH, n_phys, PAGE_SIZE, A_HEADS, A_HEAD_DIM)),
        "cache_mem_k": nrm(ks[5], (DEPTH, DEC_BATCH, N_MEM, M_HEADS, M_HEAD_DIM)),
        "cache_mem_v": nrm(ks[6], (DEPTH, DEC_BATCH, N_MEM, M_HEADS, M_HEAD_DIM)),
        "state_pool": nrm(ks[7], (DEPTH, DEC_BATCH, POOL_CTX, POOL_WIDTH)),
        "page_table": page_table,
        "mem_prompt": nrm(ks[8], (BATCH, N_MEM, D_MODEL)),
        "w_in": nrm(ks[9], (DEPTH, D_MODEL, IN_COLS), D_MODEL ** -0.5),
        "w_mem_kv": nrm(ks[10], (DEPTH, D_MODEL, 2 * M_WIDTH), D_MODEL ** -0.5),
        "w_pool": nrm(ks[11], (DEPTH, POOL_GROUPS, POOL_GC, POOL_GC), POOL_GC ** -0.5),
        "pool_scale": 1.0 + nrm(ks[12], (DEPTH, POOL_WIDTH), 0.02),
        "w_br_attn": nrm(ks[13], (DEPTH, A_WIDTH, D_MODEL), A_WIDTH ** -0.5),
        "w_br_pool": nrm(ks[14], (DEPTH, POOL_WIDTH, D_MODEL), POOL_WIDTH ** -0.5),
        "w_br_mem": nrm(ks[15], (DEPTH, M_WIDTH, D_MODEL), M_WIDTH ** -0.5),
        "w_o": nrm(ks[16], (DEPTH, D_MODEL, D_MODEL), DN_BETA * D_MODEL ** -0.5),
        "ln1_g": 1.0 + nrm(ks[17], (DEPTH, D_MODEL), 0.02),
        "ln1_b": nrm(ks[18], (DEPTH, D_MODEL), 0.02),
        "w_up": nrm(ks[19], (DEPTH, D_MODEL, D_FF), D_MODEL ** -0.5),
        "w_down": nrm(ks[20], (DEPTH, D_FF, D_MODEL), DN_BETA * D_FF ** -0.5),
        "ln2_g": 1.0 + nrm(ks[21], (DEPTH, D_MODEL), 0.02),
        "ln2_b": nrm(ks[22], (DEPTH, D_MODEL), 0.02),
    }


def reference(x_prompt, x_sample, cache_k, cache_v, cache_mem_k, cache_mem_v, state_pool, page_table,
              mem_prompt, w_in, w_mem_kv, w_pool, pool_scale, w_br_attn, w_br_pool, w_br_mem, w_o,
              ln1_g, ln1_b, w_up, w_down, ln2_g, ln2_b):
    past = page_table.shape[1] * cache_k.shape[2]
    hp, hs = x_prompt, x_sample
    nkp, nvp, nmkp, nmvp, npp, nks, nvs, nps = [], [], [], [], [], [], [], []
    for l in range(DEPTH):
        qp, kp, vp, up, mqp, gp = split_in(hp @ w_in[l])
        attn_p = moba_prompt(qp, kp, vp)
        mkp, mvp = mem_kv(mem_prompt, w_mem_kv[l])
        memo_p = mem_attend(mqp, mkp, mvp)
        zero_prefix = jnp.zeros((hp.shape[0], POOL_CTX, POOL_WIDTH), hp.dtype)
        pool_p, tail_p = pool_branch(up, zero_prefix, 0, w_pool[l], pool_scale[l])
        mix_p = merge_branches(attn_p, pool_p, memo_p, gp, w_br_attn[l], w_br_pool[l], w_br_mem[l], w_o[l])
        hp = post_norm_block(hp, mix_p, ln1_g[l], ln1_b[l], w_up[l], w_down[l], ln2_g[l], ln2_b[l])
        nkp.append(kp); nvp.append(vp); nmkp.append(mkp); nmvp.append(mvp); npp.append(tail_p)
        qs, ks_, vs, us, mqs, gs = split_in(hs @ w_in[l])
        attn_s = moba_sample(qs, ks_, vs, cache_k, cache_v, page_table, l)
        memo_s = mem_attend(mqs, cache_mem_k[l], cache_mem_v[l])
        pool_s, tail_s = pool_branch(us, state_pool[l], past, w_pool[l], pool_scale[l])
        mix_s = merge_branches(attn_s, pool_s, memo_s, gs, w_br_attn[l], w_br_pool[l], w_br_mem[l], w_o[l])
        hs = post_norm_block(hs, mix_s, ln1_g[l], ln1_b[l], w_up[l], w_down[l], ln2_g[l], ln2_b[l])
        nks.append(ks_); nvs.append(vs); nps.append(tail_s)
    return (hp, hs,
            jnp.stack(nkp, 0), jnp.stack(nvp, 0), jnp.stack(nmkp, 0), jnp.stack(nmvp, 0), jnp.stack(npp, 0),
            jnp.stack(nks, 0), jnp.stack(nvs, 0), jnp.stack(nps, 0))
```

```python
import functools

import jax
import jax.numpy as jnp
from jax import lax
from jax.experimental import pallas as pl
from jax.experimental.pallas import tpu as pltpu

MOBA_BLOCK = 256
MOBA_TOPK = 3
POOL_WINDOWS = (2, 4, 8, 16)
N_BRANCH = 3
LN_EPS = 1e-5
NEG = -1e30
PAGE_ROWS = 128
PAGES_PER_STEP = 16

V7X_VMEM_LIMIT_BYTES = 60000 * 1024
V7X_LANES = 128
COL_TILE = 1024

BF16 = jnp.bfloat16
F32 = jnp.float32


def _params(*semantics):
    return pltpu.CompilerParams(dimension_semantics=semantics,
                                vmem_limit_bytes=V7X_VMEM_LIMIT_BYTES)


def _layer_norm(x, g, b):
    mu = jnp.mean(x, axis=-1, keepdims=True)
    xc = x - mu
    var = jnp.mean(xc * xc, axis=-1, keepdims=True)
    return xc * lax.rsqrt(var + LN_EPS) * g + b


def _mm_kernel(x_ref, w_ref, o_ref):
    o_ref[0] = jnp.dot(x_ref[...], w_ref[...], preferred_element_type=F32)


def _matmul_tiles(x, w, *, tm, tn, name):
    m, k = x.shape
    n = w.shape[1]
    return pl.pallas_call(
        _mm_kernel,
        out_shape=jax.ShapeDtypeStruct((n // tn, m, tn), F32),
        grid=(n // tn, m // tm),
        in_specs=[pl.BlockSpec((tm, k), lambda j, i: (i, 0)),
                  pl.BlockSpec((k, tn), lambda j, i: (0, j))],
        out_specs=pl.BlockSpec((1, tm, tn), lambda j, i: (j, i, 0)),
        compiler_params=_params("arbitrary", "arbitrary"),
        name=name,
    )(x, w)


def _moba_prompt_kernel(q_ref, k_ref, v_ref, o_ref, *, seq, blk, topk):
    nb = seq // blk
    hd = q_ref.shape[-1]
    q = q_ref[0]
    k = k_ref[0]
    v = v_ref[0]
    means = jnp.concatenate(
        [jnp.mean(k[n * blk:(n + 1) * blk], axis=0, keepdims=True) for n in range(nb)], axis=0)
    gate = lax.dot_general(q, means, (((1,), (1,)), ((), ())),
                           precision=lax.Precision.HIGHEST, preferred_element_type=F32)
    qb = (q * (hd ** -0.5)).astype(BF16)
    kb = k.astype(BF16)
    vb = v.astype(BF16)
    row = lax.broadcasted_iota(jnp.int32, (blk, blk), 0)
    col = lax.broadcasted_iota(jnp.int32, (blk, blk), 1)
    causal = col <= row
    for i in range(nb):
        qi = qb[i * blk:(i + 1) * blk]
        gi = gate[i * blk:(i + 1) * blk]
        s = lax.dot_general(qi, kb[:(i + 1) * blk], (((1,), (1,)), ((), ())),
                            preferred_element_type=F32)
        parts = []
        for n in range(i):
            sn = s[:, n * blk:(n + 1) * blk]
            if i > topk:
                gn = gi[:, n:n + 1]
                beaten = jnp.zeros((blk, 1), jnp.int32)
                for m in range(i):
                    if m == n:
                        continue
                    gm = gi[:, m:m + 1]
                    wins = (gm >= gn) if m < n else (gm > gn)
                    beaten = beaten + wins.astype(jnp.int32)
                sn = jnp.where(beaten < topk, sn, NEG)
            parts.append(sn)
        parts.append(jnp.where(causal, s[:, i * blk:(i + 1) * blk], NEG))
        s = jnp.concatenate(parts, axis=1) if len(parts) > 1 else parts[0]
        mx = jnp.max(s, axis=-1, keepdims=True)
        p = jnp.exp(s - mx)
        den = jnp.sum(p, axis=-1, keepdims=True)
        o = jnp.dot(p.astype(BF16), vb[:(i + 1) * blk], preferred_element_type=F32)
        o_ref[i * blk:(i + 1) * blk, :] = (o / den).astype(o_ref.dtype)


def _moba_prompt(p3, *, batch, seq, heads, hd):
    assert heads * hd == COL_TILE
    kern = functools.partial(_moba_prompt_kernel, seq=seq, blk=MOBA_BLOCK, topk=MOBA_TOPK)
    spec = lambda c: pl.BlockSpec((1, seq, hd), lambda b, h: (c, b, h))
    return pl.pallas_call(
        kern,
        out_shape=jax.ShapeDtypeStruct((batch * seq, heads * hd), BF16),
        grid=(batch, heads),
        in_specs=[spec(0), spec(1), spec(2)],
        out_specs=pl.BlockSpec((seq, hd), lambda b, h: (b, h)),
        compiler_params=_params("arbitrary", "arbitrary"),
        name="moba_prompt",
    )(p3, p3, p3)


def _pool_prompt_kernel(u_ref, w_ref, sc_ref, o_ref, *, windows):
    seq = u_ref.shape[1]
    gc = w_ref.shape[-1]
    t = lax.broadcasted_iota(jnp.int32, (seq, 1), 0)
    for g, w in enumerate(windows):
        x = u_ref[0, :, g * gc:(g + 1) * gc]
        win = x
        width = 1
        while width < w:
            shifted = pltpu.roll(win, width, 0)
            win = win + jnp.where(t >= width, shifted, 0.0)
            width *= 2
        cnt = jnp.minimum(w, t + 1).astype(F32)
        mix = win / cnt - x
        y = jnp.dot(mix.astype(BF16), w_ref[g], preferred_element_type=F32)
        o_ref[:, g * gc:(g + 1) * gc] = (y * sc_ref[:, g * gc:(g + 1) * gc]).astype(o_ref.dtype)


def _pool_prompt(p3, w_pool, scale, *, batch, seq, u_col):
    groups, gc, _ = w_pool.shape
    width = groups * gc
    assert width == COL_TILE
    kern = functools.partial(_pool_prompt_kernel, windows=POOL_WINDOWS)
    return pl.pallas_call(
        kern,
        out_shape=jax.ShapeDtypeStruct((batch * seq, width), BF16),
        grid=(batch,),
        in_specs=[pl.BlockSpec((1, seq, width), lambda b: (u_col, b, 0)),
                  pl.BlockSpec((groups, gc, gc), lambda b: (0, 0, 0)),
                  pl.BlockSpec((1, width), lambda b: (0, 0))],
        out_specs=pl.BlockSpec((seq, width), lambda b: (b, 0)),
        compiler_params=_params("arbitrary"),
        name="pool_prompt",
    )(p3, w_pool, scale)


def _mem_prompt_kernel(q_ref, k_ref, v_ref, o_ref):
    hd = q_ref.shape[-1]
    qb = (q_ref[0] * (hd ** -0.5)).astype(BF16)
    s = lax.dot_general(qb, k_ref[0].astype(BF16), (((1,), (1,)), ((), ())),
                        preferred_element_type=F32)
    mx = jnp.max(s, axis=-1, keepdims=True)
    p = jnp.exp(s - mx)
    den = jnp.sum(p, axis=-1, keepdims=True)
    o = jnp.dot(p.astype(BF16), v_ref[0].astype(BF16), preferred_element_type=F32)
    o_ref[...] = (o / den).astype(o_ref.dtype)


def _mem_prompt(p3, kv3, *, batch, seq, n_mem, heads, hd, q_col):
    assert heads * hd == COL_TILE
    return pl.pallas_call(
        _mem_prompt_kernel,
        out_shape=jax.ShapeDtypeStruct((batch * seq, heads * hd), BF16),
        grid=(batch, heads),
        in_specs=[pl.BlockSpec((1, seq, hd), lambda b, h: (q_col, b, h)),
                  pl.BlockSpec((1, n_mem, hd), lambda b, h: (0, b, h)),
                  pl.BlockSpec((1, n_mem, hd), lambda b, h: (1, b, h))],
        out_specs=pl.BlockSpec((seq, hd), lambda b, h: (b, h)),
        compiler_params=_params("arbitrary", "arbitrary"),
        name="mem_prompt",
    )(p3, kv3, kv3)


def _merge_kernel(a_ref, p_ref, m_ref, wa_ref, wp_ref, wm_ref, ga_ref, gp_ref, gm_ref, o_ref):
    def branch(x_ref, w_ref, g_ref):
        y = jnp.dot(x_ref[...].astype(BF16), w_ref[...], preferred_element_type=F32)
        return jax.nn.sigmoid(g_ref[0]) * y
    mix = branch(a_ref, wa_ref, ga_ref) + branch(p_ref, wp_ref, gp_ref) + branch(m_ref, wm_ref, gm_ref)
    o_ref[...] = mix.astype(o_ref.dtype)


def _merge(attn, pool, memo, w_ba, w_bp, w_bm, p3, *, gate_col, tm):
    m, kw = attn.shape
    d = w_ba.shape[1]
    tn = COL_TILE
    per_gate = d // tn
    lhs = pl.BlockSpec((tm, kw), lambda j, i: (i, 0))
    wsp = pl.BlockSpec((kw, tn), lambda j, i: (0, j))
    gsp = lambda c: pl.BlockSpec((1, tm, tn), lambda j, i: (gate_col + c * per_gate + j, i, 0))
    return pl.pallas_call(
        _merge_kernel,
        out_shape=jax.ShapeDtypeStruct((m, d), BF16),
        grid=(d // tn, m // tm),
        in_specs=[lhs, lhs, lhs, wsp, wsp, wsp, gsp(0), gsp(1), gsp(2)],
        out_specs=pl.BlockSpec((tm, tn), lambda j, i: (i, j)),
        compiler_params=_params("arbitrary", "arbitrary"),
        name="merge",
    )(attn, pool, memo, w_ba, w_bp, w_bm, p3, p3, p3)


def _oproj_kernel(x_ref, mix_ref, w_ref, g_ref, b_ref, o_ref, *, alpha):
    y = jnp.dot(mix_ref[...], w_ref[...], preferred_element_type=F32)
    o_ref[...] = _layer_norm(alpha * x_ref[...] + y, g_ref[...], b_ref[...])


def _oproj(x, mix, w_o, g, b, *, alpha, tm):
    m, d = x.shape
    row = pl.BlockSpec((tm, d), lambda i: (i, 0))
    vec = pl.BlockSpec((1, d), lambda i: (0, 0))
    return pl.pallas_call(
        functools.partial(_oproj_kernel, alpha=alpha),
        out_shape=jax.ShapeDtypeStruct((m, d), F32),
        grid=(m // tm,),
        in_specs=[row, row, pl.BlockSpec((d, d), lambda i: (0, 0)), vec, vec],
        out_specs=row,
        compiler_params=_params("arbitrary"),
        name="oproj_ln1",
    )(x, mix, w_o, g, b)


def _ffn_kernel(h_ref, wu_ref, wd_ref, g_ref, b_ref, o_ref, hb_ref, acc_ref, *, alpha):
    kk = pl.program_id(1)

    @pl.when(kk == 0)
    def _():
        hb_ref[...] = h_ref[...].astype(BF16)
        acc_ref[...] = jnp.zeros_like(acc_ref)

    a = jnp.dot(hb_ref[...], wu_ref[...], preferred_element_type=F32)
    a = jnp.square(jnp.maximum(a, 0.0)).astype(BF16)
    acc_ref[...] += jnp.dot(a, wd_ref[...], preferred_element_type=F32)

    @pl.when(kk == pl.num_programs(1) - 1)
    def _():
        o_ref[...] = _layer_norm(alpha * h_ref[...] + acc_ref[...], g_ref[...], b_ref[...])


def _ffn(h, w_up, w_down, g, b, *, alpha, tm, tk):
    m, d = h.shape
    dff = w_up.shape[1]
    row = pl.BlockSpec((tm, d), lambda i, k: (i, 0))
    vec = pl.BlockSpec((1, d), lambda i, k: (0, 0))
    return pl.pallas_call(
        functools.partial(_ffn_kernel, alpha=alpha),
        out_shape=jax.ShapeDtypeStruct((m, d), F32),
        grid=(m // tm, dff // tk),
        in_specs=[row,
                  pl.BlockSpec((d, tk), lambda i, k: (0, k)),
                  pl.BlockSpec((tk, d), lambda i, k: (k, 0)),
                  vec, vec],
        out_specs=row,
        scratch_shapes=[pltpu.VMEM((tm, d), BF16), pltpu.VMEM((tm, d), F32)],
        compiler_params=_params("arbitrary", "arbitrary"),
        name="ffn_ln2",
    )(h, w_up, w_down, g, b)


def _block_sum_kernel(pt_ref, *refs, per_blk):
    del pt_ref
    o_ref = refs[-1]
    pages = refs[:-1]
    for n in range(len(pages) // per_blk):
        tot = jnp.sum(pages[n * per_blk][0, 0], axis=0)
        for r in range(1, per_blk):
            tot = tot + jnp.sum(pages[n * per_blk + r][0, 0], axis=0)
        o_ref[0, n] = tot


def _block_sums(cache, page_table):
    _, _, page, heads, hd = cache.shape
    bsz, n_pages = page_table.shape
    per_blk = MOBA_BLOCK // page
    chunks = n_pages // PAGES_PER_STEP
    blocks_per_step = PAGES_PER_STEP // per_blk

    def spec(r):
        return pl.BlockSpec((1, 1, page, heads, hd),
                            lambda b, c, pt: (0, pt[b, c * PAGES_PER_STEP + r], 0, 0, 0))

    return pl.pallas_call(
        functools.partial(_block_sum_kernel, per_blk=per_blk),
        out_shape=jax.ShapeDtypeStruct((bsz, n_pages // per_blk, heads, hd), F32),
        grid_spec=pltpu.PrefetchScalarGridSpec(
            num_scalar_prefetch=1,
            grid=(bsz, chunks),
            in_specs=[spec(r) for r in range(PAGES_PER_STEP)],
            out_specs=pl.BlockSpec((1, blocks_per_step, heads, hd), lambda b, c, pt: (b, c, 0, 0)),
        ),
        compiler_params=_params("arbitrary", "arbitrary"),
        name="block_sums",
    )(page_table, *([cache] * PAGES_PER_STEP))


def _select_kernel(bs_ref, q_ref, kn_ref, idx_ref, *, blk, topk):
    n_past = bs_ref.shape[1]
    q = q_ref[0]
    past = jnp.sum((bs_ref[0] / blk) * q, axis=-1, keepdims=True)
    own = jnp.sum((kn_ref[0] / blk) * q, axis=-1, keepdims=True)[None]
    gate = jnp.concatenate([past, own], axis=0)
    n = lax.broadcasted_iota(jnp.int32, gate.shape, 0)
    gate = jnp.where(n < n_past, gate, NEG)
    picks = []
    for _ in range(topk):
        best = jnp.max(gate, axis=0, keepdims=True)
        first = jnp.min(jnp.where(gate == best, n, n_past + 1), axis=0, keepdims=True)
        picks.append(first)
        gate = jnp.where(n == first, -jnp.inf, gate)
    idx = jnp.concatenate(picks, axis=0)
    idx_ref[0] = jnp.broadcast_to(idx, idx_ref.shape[1:])


def _select_blocks(bsum, q, k_new):
    bsz, n_past, heads, hd = bsum.shape
    kern = functools.partial(_select_kernel, blk=MOBA_BLOCK, topk=MOBA_TOPK)
    tok = pl.BlockSpec((1, heads, hd), lambda b: (b, 0, 0))
    return pl.pallas_call(
        kern,
        out_shape=jax.ShapeDtypeStruct((bsz, MOBA_TOPK, heads, V7X_LANES), jnp.int32),
        grid=(bsz,),
        in_specs=[pl.BlockSpec((1, n_past, heads, hd), lambda b: (b, 0, 0, 0)), tok, tok],
        out_specs=pl.BlockSpec((1, MOBA_TOPK, heads, V7X_LANES), lambda b: (b, 0, 0, 0)),
        compiler_params=_params("arbitrary"),
        name="moba_select",
    )(bsum, q, k_new)


def _moba_sample_kernel(pt_ref, idx_ref, q_ref, kn_ref, vn_ref, *refs):
    del pt_ref, idx_ref
    o_ref = refs[-1]
    n_sel = (len(refs) - 1) // 2
    hd = q_ref.shape[-1]
    q = q_ref[0, 0] * (hd ** -0.5)
    ks = jnp.concatenate([r[0, :, 0, 0, :] for r in refs[:n_sel]], axis=0)
    vs = jnp.concatenate([r[0, :, 0, 0, :] for r in refs[n_sel:2 * n_sel]], axis=0)
    q8 = jnp.broadcast_to(q, (8, hd)).astype(BF16)
    s_sel = lax.dot_general(q8, ks.astype(BF16), (((1,), (1,)), ((), ())),
                            preferred_element_type=F32)
    s_own = jnp.sum(q * kn_ref[0, 0], axis=-1, keepdims=True)
    mx = jnp.maximum(jnp.max(s_sel, axis=-1, keepdims=True), s_own)
    p_sel = jnp.exp(s_sel - mx)
    p_own = jnp.exp(s_own - mx)
    den = jnp.sum(p_sel, axis=-1, keepdims=True) + p_own
    o = jnp.dot(p_sel.astype(BF16), vs.astype(BF16), preferred_element_type=F32)
    o = (o + p_own * vn_ref[0, 0]) / den
    o_ref[0, 0] = o[0:1]


def _moba_sample(cache_k, cache_v, page_table, idx, q, k_new, v_new, *, heads, hd):
    bsz = q.shape[0]
    per_blk = MOBA_BLOCK // PAGE_ROWS

    def cache_spec(j, r):
        def index(b, h, pt, ix):
            blk = ix[(b * heads + h) * MOBA_TOPK + j]
            return (pt[b, blk * per_blk + r], 0, h, 0, 0)
        return pl.BlockSpec((1, PAGE_ROWS, 1, 1, hd), index)

    tok = pl.BlockSpec((1, 1, 1, hd), lambda b, h, pt, ix: (b, h, 0, 0))
    sel_specs = [cache_spec(j, r) for j in range(MOBA_TOPK) for r in range(per_blk)]
    shape4 = (bsz, heads, 1, hd)
    out = pl.pallas_call(
        _moba_sample_kernel,
        out_shape=jax.ShapeDtypeStruct(shape4, F32),
        grid_spec=pltpu.PrefetchScalarGridSpec(
            num_scalar_prefetch=2,
            grid=(bsz, heads),
            in_specs=[tok, tok, tok] + sel_specs + sel_specs,
            out_specs=tok,
        ),
        compiler_params=_params("arbitrary", "arbitrary"),
        name="moba_sample",
    )(page_table, idx, q.reshape(shape4), k_new.reshape(shape4), v_new.reshape(shape4),
      *([cache_k] * len(sel_specs)), *([cache_v] * len(sel_specs)))
    return out.reshape(bsz, heads * hd)


def _mem_sample_kernel(q_ref, k_ref, v_ref, o_ref, *, heads):
    hd = q_ref.shape[-1] // heads
    for h in range(heads):
        sl = slice(h * hd, (h + 1) * hd)
        q = q_ref[0, :, sl] * (hd ** -0.5)
        s = jnp.sum(k_ref[0, :, sl] * q, axis=-1, keepdims=True)
        mx = jnp.max(s, axis=0, keepdims=True)
        p = jnp.exp(s - mx)
        den = jnp.sum(p, axis=0, keepdims=True)
        o = jnp.sum(p * v_ref[0, :, sl], axis=0, keepdims=True)
        o_ref[0, :, sl] = o / den


def _mem_sample(q, mem_k, mem_v, *, heads):
    bsz, n_mem, width = mem_k.shape
    tok = pl.BlockSpec((1, 1, width), lambda b: (b, 0, 0))
    mem = pl.BlockSpec((1, n_mem, width), lambda b: (b, 0, 0))
    out = pl.pallas_call(
        functools.partial(_mem_sample_kernel, heads=heads),
        out_shape=jax.ShapeDtypeStruct((bsz, 1, width), F32),
        grid=(bsz,),
        in_specs=[tok, mem, mem],
        out_specs=tok,
        compiler_params=_params("arbitrary"),
        name="mem_sample",
    )(q.reshape(bsz, 1, width), mem_k, mem_v)
    return out.reshape(bsz, width)


def _pool_sample_kernel(u_ref, st_ref, w_ref, sc_ref, o_ref, *, windows):
    ctx = st_ref.shape[0]
    gc = w_ref.shape[-1]
    for g, w in enumerate(windows):
        sl = slice(g * gc, (g + 1) * gc)
        x = u_ref[:, sl]
        win = x
        for j in range(1, w):
            win = win + st_ref[ctx - j, :, sl]
        mix = win / float(w) - x
        y = jnp.dot(mix.astype(BF16), w_ref[g], preferred_element_type=F32)
        o_ref[:, sl] = (y * sc_ref[:, sl]).astype(o_ref.dtype)


def _pool_sample(u, state, w_pool, scale):
    bsz, width = u.shape
    return pl.pallas_call(
        functools.partial(_pool_sample_kernel, windows=POOL_WINDOWS),
        out_shape=jax.ShapeDtypeStruct((bsz, width), BF16),
        name="pool_sample",
        compiler_params=pltpu.CompilerParams(vmem_limit_bytes=V7X_VMEM_LIMIT_BYTES),
    )(u, state, w_pool, scale)


def kernel(x_prompt, x_sample, cache_k, cache_v, cache_mem_k, cache_mem_v, state_pool, page_table, mem_prompt, w_in, w_mem_kv, w_pool, pool_scale, w_br_attn, w_br_pool, w_br_mem, w_o, ln1_g, ln1_b, w_up, w_down, ln2_g, ln2_b):
    depth = w_in.shape[0]
    assert depth == 1, "single-layer trunk"
    batch, seq, d_model = x_prompt.shape
    dec_batch, dec_seq, _ = x_sample.shape
    assert dec_seq == 1
    _, n_phys, page, a_heads, a_hd = cache_k.shape
    assert page == PAGE_ROWS
    _, _, n_mem, m_heads, m_hd = cache_mem_k.shape
    pool_ctx, pool_width = state_pool.shape[2], state_pool.shape[3]
    a_width = a_heads * a_hd
    m_width = m_heads * m_hd
    alpha = float((2 * depth) ** 0.25)
    assert a_width == pool_width == m_width == COL_TILE
    u_col, mq_col, gate_col = 3, 4, 5

    w_in_b = w_in[0].astype(BF16)
    w_ba = w_br_attn[0].astype(BF16)
    w_bp = w_br_pool[0].astype(BF16)
    w_bm = w_br_mem[0].astype(BF16)
    w_o_b = w_o[0].astype(BF16)
    w_up_b = w_up[0].astype(BF16)
    w_down_b = w_down[0].astype(BF16)
    w_pool_b = w_pool[0].astype(BF16)
    w_mkv_b = w_mem_kv[0].astype(BF16)

    mp = batch * seq
    xp = x_prompt.reshape(mp, d_model)
    p3 = _matmul_tiles(xp.astype(BF16), w_in_b, tm=1024, tn=COL_TILE, name="in_proj_prompt")
    attn_p = _moba_prompt(p3, batch=batch, seq=seq, heads=a_heads, hd=a_hd)
    pool_p = _pool_prompt(p3, w_pool_b, pool_scale, batch=batch, seq=seq, u_col=u_col)
    kv3 = _matmul_tiles(mem_prompt.reshape(batch * n_mem, d_model).astype(BF16), w_mkv_b,
                        tm=batch * n_mem, tn=COL_TILE, name="mem_kv_prompt")
    memo_p = _mem_prompt(p3, kv3, batch=batch, seq=seq, n_mem=n_mem, heads=m_heads, hd=m_hd, q_col=mq_col)
    mix_p = _merge(attn_p, pool_p, memo_p, w_ba, w_bp, w_bm, p3, gate_col=gate_col, tm=512)
    h_p = _oproj(xp, mix_p, w_o_b, ln1_g, ln1_b, alpha=alpha, tm=512)
    y_p = _ffn(h_p, w_up_b, w_down_b, ln2_g, ln2_b, alpha=alpha, tm=512, tk=512)

    xs = x_sample.reshape(dec_batch, d_model)
    p3s = _matmul_tiles(xs.astype(BF16), w_in_b, tm=dec_batch, tn=COL_TILE, name="in_proj_sample")
    q_s, k_s, v_s, u_s, mq_s = p3s[0], p3s[1], p3s[2], p3s[u_col], p3s[mq_col]
    bsum = _block_sums(cache_k, page_table)
    head_shape = (dec_batch, a_heads, a_hd)
    picks = _select_blocks(bsum, q_s.reshape(head_shape), k_s.reshape(head_shape))
    idx = jnp.transpose(picks[:, :, :, 0], (0, 2, 1)).reshape(-1)
    row_tiled = (n_phys, page, a_heads, 1, a_hd)
    attn_s = _moba_sample(cache_k.reshape(row_tiled), cache_v.reshape(row_tiled), page_table, idx,
                          q_s, k_s, v_s, heads=a_heads, hd=a_hd)
    memo_s = _mem_sample(mq_s, cache_mem_k[0].reshape(dec_batch, n_mem, m_width),
                         cache_mem_v[0].reshape(dec_batch, n_mem, m_width), heads=m_heads)
    pool_s = _pool_sample(u_s, jnp.transpose(state_pool[0], (1, 0, 2)), w_pool_b, pool_scale)
    mix_s = _merge(attn_s, pool_s, memo_s, w_ba, w_bp, w_bm, p3s, gate_col=gate_col, tm=dec_batch)
    h_s = _oproj(xs, mix_s, w_o_b, ln1_g, ln1_b, alpha=alpha, tm=dec_batch)
    y_s = _ffn(h_s, w_up_b, w_down_b, ln2_g, ln2_b, alpha=alpha, tm=dec_batch, tk=2048)

    kv_shape = (depth, batch, seq, a_heads, a_hd)
    mem_shape = (depth, batch, n_mem, m_heads, m_hd)
    u_p = p3[u_col].reshape(batch, seq, pool_width)
    new_pool_p = u_p[:, seq - pool_ctx:, :][None]
    new_pool_s = jnp.concatenate([state_pool[0][:, 1:, :], u_s[:, None, :]], axis=1)[None]
    skv_shape = (depth, dec_batch, dec_seq, a_heads, a_hd)
    return (y_p.reshape(batch, seq, d_model), y_s.reshape(dec_batch, dec_seq, d_model),
            p3[1].reshape(kv_shape), p3[2].reshape(kv_shape),
            kv3[0].reshape(mem_shape), kv3[1].reshape(mem_shape),
            new_pool_p,
            k_s.reshape(skv_shape), v_s.reshape(skv_shape),
            new_pool_s)
```

```python
import functools

import jax
import jax.numpy as jnp
from jax import lax
from jax.experimental import pallas as pl
from jax.experimental.pallas import tpu as pltpu

MOBA_BLOCK = 256
MOBA_TOPK = 3
POOL_WINDOWS = (2, 4, 8, 16)
N_BRANCH = 3
LN_EPS = 1e-5
NEG = -1e30
PAGE_ROWS = 128
PAGES_PER_STEP = 16

V7X_VMEM_LIMIT_BYTES = 60000 * 1024
V7X_LANES = 128
COL_TILE = 1024

BF16 = jnp.bfloat16
F32 = jnp.float32
NT = (((1,), (1,)), ((), ()))


def _params(*semantics):
    return pltpu.CompilerParams(dimension_semantics=semantics,
                                vmem_limit_bytes=V7X_VMEM_LIMIT_BYTES)


def _layer_norm(x, g, b):
    mu = jnp.mean(x, axis=-1, keepdims=True)
    xc = x - mu
    var = jnp.mean(xc * xc, axis=-1, keepdims=True)
    return xc * lax.rsqrt(var + LN_EPS) * g + b


def _proj_kernel(x_ref, w_ref, o_ref, *rest, heads):
    wb_ref = rest[-1]

    @pl.when(pl.program_id(1) == 0)
    def _():
        wb_ref[...] = w_ref[...].astype(BF16)

    y = jnp.dot(x_ref[...].astype(BF16), wb_ref[...], preferred_element_type=F32)
    o_ref[0] = y
    if heads:
        oh_ref = rest[0]
        tm, hd = y.shape[0], y.shape[1] // heads
        for h in range(heads):
            oh_ref[pl.ds(h, tm, stride=heads), :] = y[:, h * hd:(h + 1) * hd]


def _project(x, w, col_of, n_cols, *, tm, name, heads=0):
    m, k = x.shape
    tn = COL_TILE
    out_shape = [jax.ShapeDtypeStruct((n_cols, m, tn), F32)]
    out_specs = [pl.BlockSpec((1, tm, tn), lambda j, i: (j, i, 0))]
    if heads:
        assert n_cols == 1
        out_shape.append(jax.ShapeDtypeStruct((m * heads, tn // heads), F32))
        out_specs.append(pl.BlockSpec((tm * heads, tn // heads), lambda j, i: (i, 0)))
    return pl.pallas_call(
        functools.partial(_proj_kernel, heads=heads),
        out_shape=out_shape,
        grid=(n_cols, m // tm),
        in_specs=[pl.BlockSpec((tm, k), lambda j, i: (i, 0)),
                  pl.BlockSpec((k, tn), lambda j, i: (0, col_of(j)))],
        out_specs=out_specs,
        scratch_shapes=[pltpu.VMEM((k, tn), BF16)],
        compiler_params=_params("arbitrary", "arbitrary"),
        name=name,
    )(x, w)


def _moba_prompt_kernel(q_ref, k_ref, v_ref, o_ref, *, seq, blk, topk):
    nb = seq // blk
    hd = q_ref.shape[-1]
    q = q_ref[0]
    k = k_ref[...]
    means = jnp.concatenate(
        [jnp.mean(k[n * blk:(n + 1) * blk], axis=0, keepdims=True) for n in range(nb)], axis=0)
    qs = q * (hd ** -0.5)
    kb = k.astype(BF16)
    vb = v_ref[...].astype(BF16)
    key_blk = lax.broadcasted_iota(jnp.int32, (seq, hd), 0) // blk
    lane = lax.broadcasted_iota(jnp.int32, (seq, hd), 1)
    k_ext = jnp.concatenate([kb, jnp.where(key_blk == lane, 1.0, 0.0).astype(BF16)], axis=1)
    row = lax.broadcasted_iota(jnp.int32, (blk, blk), 0)
    col = lax.broadcasted_iota(jnp.int32, (blk, blk), 1)
    causal = col <= row
    blk_id = lax.broadcasted_iota(jnp.int32, (nb, blk), 0)
    for i in range(nb):
        qi = qs[i * blk:(i + 1) * blk]
        if i > topk:
            gate = lax.dot_general(means, q[i * blk:(i + 1) * blk], NT,
                                   precision=lax.Precision.HIGHEST, preferred_element_type=F32)
            beaten = jnp.zeros((nb, blk), jnp.int32)
            for m in range(i):
                gm = gate[m:m + 1, :]
                beaten = beaten + jnp.where(m < blk_id, jnp.where(gm >= gate, 1, 0), jnp.where(gm > gate, 1, 0))
            bias = jnp.where((beaten < topk) | (blk_id >= i), 0.0, NEG)
            bias = jnp.concatenate([bias, jnp.zeros((hd - nb, blk), F32)], axis=0).T
            q_ext = jnp.concatenate([qi.astype(BF16), bias.astype(BF16)], axis=1)
            s = lax.dot_general(q_ext, k_ext[:(i + 1) * blk], NT, preferred_element_type=F32)
        else:
            s = lax.dot_general(qi.astype(BF16), kb[:(i + 1) * blk], NT, preferred_element_type=F32)
        own = jnp.where(causal, s[:, i * blk:(i + 1) * blk], NEG)
        s = jnp.concatenate([s[:, :i * blk], own], axis=1) if i else own
        mx = jnp.max(s, axis=-1, keepdims=True)
        p = jnp.exp(s - mx)
        den = jnp.sum(p, axis=-1, keepdims=True)
        o = jnp.dot(p.astype(BF16), vb[:(i + 1) * blk], preferred_element_type=F32)
        o_ref[i * blk:(i + 1) * blk, :] = (o / den).astype(o_ref.dtype)


def _moba_prompt(p3, k2, v2, *, batch, seq, heads, hd, q_col):
    assert heads * hd == COL_TILE
    kern = functools.partial(_moba_prompt_kernel, seq=seq, blk=MOBA_BLOCK, topk=MOBA_TOPK)
    kv = pl.BlockSpec((seq, hd), lambda b, h: (b, h))
    return pl.pallas_call(
        kern,
        out_shape=jax.ShapeDtypeStruct((batch * seq, heads * hd), BF16),
        grid=(batch, heads),
        in_specs=[pl.BlockSpec((1, seq, hd), lambda b, h: (q_col, b, h)), kv, kv],
        out_specs=kv,
        compiler_params=_params("arbitrary", "arbitrary"),
        name="moba_prompt",
    )(p3, k2, v2)


def _pool_prompt_kernel(u_ref, w_ref, sc_ref, o_ref, *, windows):
    seq = u_ref.shape[1]
    gc = w_ref.shape[-1]
    t = lax.broadcasted_iota(jnp.int32, (seq, 1), 0)
    for g, w in enumerate(windows):
        x = u_ref[0, :, g * gc:(g + 1) * gc]
        win = x
        width = 1
        while width < w:
            shifted = pltpu.roll(win, width, 0)
            win = win + jnp.where(t >= width, shifted, 0.0)
            width *= 2
        cnt = jnp.minimum(w, t + 1).astype(F32)
        mix = win / cnt - x
        y = jnp.dot(mix.astype(BF16), w_ref[g], preferred_element_type=F32)
        o_ref[:, g * gc:(g + 1) * gc] = (y * sc_ref[:, g * gc:(g + 1) * gc]).astype(o_ref.dtype)


def _pool_prompt(p3, w_pool, scale, *, batch, seq, u_col):
    groups, gc, _ = w_pool.shape
    width = groups * gc
    assert width == COL_TILE
    kern = functools.partial(_pool_prompt_kernel, windows=POOL_WINDOWS)
    return pl.pallas_call(
        kern,
        out_shape=jax.ShapeDtypeStruct((batch * seq, width), BF16),
        grid=(batch,),
        in_specs=[pl.BlockSpec((1, seq, width), lambda b: (u_col, b, 0)),
                  pl.BlockSpec((groups, gc, gc), lambda b: (0, 0, 0)),
                  pl.BlockSpec((1, width), lambda b: (0, 0))],
        out_specs=pl.BlockSpec((seq, width), lambda b: (b, 0)),
        compiler_params=_params("arbitrary"),
        name="pool_prompt",
    )(p3, w_pool, scale)


def _mem_prompt_kernel(q_ref, k_ref, v_ref, o_ref):
    hd = q_ref.shape[-1]
    qb = (q_ref[0] * (hd ** -0.5)).astype(BF16)
    s = lax.dot_general(qb, k_ref[0].astype(BF16), NT, preferred_element_type=F32)
    mx = jnp.max(s, axis=-1, keepdims=True)
    p = jnp.exp(s - mx)
    den = jnp.sum(p, axis=-1, keepdims=True)
    o = jnp.dot(p.astype(BF16), v_ref[0].astype(BF16), preferred_element_type=F32)
    o_ref[...] = (o / den).astype(o_ref.dtype)


def _mem_prompt(p3, kv3, *, batch, seq, n_mem, heads, hd, q_col):
    assert heads * hd == COL_TILE
    return pl.pallas_call(
        _mem_prompt_kernel,
        out_shape=jax.ShapeDtypeStruct((batch * seq, heads * hd), BF16),
        grid=(batch, heads),
        in_specs=[pl.BlockSpec((1, seq, hd), lambda b, h: (q_col, b, h)),
                  pl.BlockSpec((1, n_mem, hd), lambda b, h: (0, b, h)),
                  pl.BlockSpec((1, n_mem, hd), lambda b, h: (1, b, h))],
        out_specs=pl.BlockSpec((seq, hd), lambda b, h: (b, h)),
        compiler_params=_params("arbitrary", "arbitrary"),
        name="mem_prompt",
    )(p3, kv3, kv3)


def _merge_kernel(a_ref, p_ref, m_ref, wa_ref, wp_ref, wm_ref, ga_ref, gp_ref, gm_ref, o_ref,
                  wab_ref, wpb_ref, wmb_ref):
    @pl.when(pl.program_id(1) == 0)
    def _():
        wab_ref[...] = wa_ref[...].astype(BF16)
        wpb_ref[...] = wp_ref[...].astype(BF16)
        wmb_ref[...] = wm_ref[...].astype(BF16)

    def branch(x_ref, w_ref, g_ref):
        y = jnp.dot(x_ref[...].astype(BF16), w_ref[...], preferred_element_type=F32)
        return jax.nn.sigmoid(g_ref[0]) * y
    mix = branch(a_ref, wab_ref, ga_ref) + branch(p_ref, wpb_ref, gp_ref) + branch(m_ref, wmb_ref, gm_ref)
    o_ref[...] = mix.astype(o_ref.dtype)


def _merge(attn, pool, memo, w_ba, w_bp, w_bm, p3, *, gate_col, tm):
    m, kw = attn.shape
    d = w_ba.shape[1]
    tn = COL_TILE
    per_gate = d // tn
    lhs = pl.BlockSpec((tm, kw), lambda j, i: (i, 0))
    wsp = pl.BlockSpec((kw, tn), lambda j, i: (0, j))
    gsp = lambda c: pl.BlockSpec((1, tm, tn), lambda j, i: (gate_col + c * per_gate + j, i, 0))
    return pl.pallas_call(
        _merge_kernel,
        out_shape=jax.ShapeDtypeStruct((m, d), BF16),
        grid=(d // tn, m // tm),
        in_specs=[lhs, lhs, lhs, wsp, wsp, wsp, gsp(0), gsp(1), gsp(2)],
        out_specs=pl.BlockSpec((tm, tn), lambda j, i: (i, j)),
        scratch_shapes=[pltpu.VMEM((kw, tn), BF16)] * N_BRANCH,
        compiler_params=_params("arbitrary", "arbitrary"),
        name="merge",
    )(attn, pool, memo, w_ba, w_bp, w_bm, p3, p3, p3)


def _oproj_kernel(x_ref, mix_ref, w_ref, g_ref, b_ref, o_ref, *, alpha):
    y = jnp.dot(mix_ref[...], w_ref[...], preferred_element_type=F32)
    o_ref[...] = _layer_norm(alpha * x_ref[...] + y, g_ref[...], b_ref[...])


def _oproj(x, mix, w_o, g, b, *, alpha, tm):
    m, d = x.shape
    row = pl.BlockSpec((tm, d), lambda i: (i, 0))
    vec = pl.BlockSpec((1, d), lambda i: (0, 0))
    return pl.pallas_call(
        functools.partial(_oproj_kernel, alpha=alpha),
        out_shape=jax.ShapeDtypeStruct((m, d), F32),
        grid=(m // tm,),
        in_specs=[row, row, pl.BlockSpec((d, d), lambda i: (0, 0)), vec, vec],
        out_specs=row,
        compiler_params=_params("arbitrary"),
        name="oproj_ln1",
    )(x, mix, w_o, g, b)


def _sum_page_blocks(pages, o_ref, per_blk):
    for n in range(len(pages) // per_blk):
        tot = jnp.sum(pages[n * per_blk][0, 0], axis=0)
        for r in range(1, per_blk):
            tot = tot + jnp.sum(pages[n * per_blk + r][0, 0], axis=0)
        o_ref[0, n] = tot


def _ffn_kernel(*refs, alpha, n_pages, per_blk):
    if n_pages:
        refs = refs[1:]
    h_ref, wu_ref, wd_ref, g_ref, b_ref = refs[:5]
    pages = refs[5:5 + n_pages]
    o_ref = refs[5 + n_pages]
    hb_ref, acc_ref = refs[-2:]
    kk = pl.program_id(1)

    @pl.when(kk == 0)
    def _():
        hb_ref[...] = h_ref[...].astype(BF16)
        acc_ref[...] = jnp.zeros_like(acc_ref)

    a = jnp.dot(hb_ref[...], wu_ref[...], preferred_element_type=F32)
    a = jnp.square(jnp.maximum(a, 0.0)).astype(BF16)
    acc_ref[...] += jnp.dot(a, wd_ref[...], preferred_element_type=F32)
    if n_pages:
        _sum_page_blocks(pages, refs[6 + n_pages], per_blk)

    @pl.when(kk == pl.num_programs(1) - 1)
    def _():
        o_ref[...] = _layer_norm(alpha * h_ref[...] + acc_ref[...], g_ref[...], b_ref[...])


def _ffn(h, w_up, w_down, g, b, *, alpha, tm, tk, cache=None, page_table=None):
    m, d = h.shape
    dff = w_up.shape[1]
    n_i, n_k = m // tm, dff // tk
    row = pl.BlockSpec((tm, d), lambda i, k, *_: (i, 0))
    vec = pl.BlockSpec((1, d), lambda i, k, *_: (0, 0))
    in_specs = [row,
                pl.BlockSpec((d, tk), lambda i, k, *_: (0, k)),
                pl.BlockSpec((tk, d), lambda i, k, *_: (k, 0)),
                vec, vec]
    out_shape = [jax.ShapeDtypeStruct((m, d), F32)]
    out_specs = [row]
    operands = [h, w_up, w_down, g, b]
    n_pages = per_blk = 0
    if cache is not None:
        _, _, page, heads, hd = cache.shape
        bsz, pages_per_row = page_table.shape
        per_blk = MOBA_BLOCK // page
        n_pages = PAGES_PER_STEP
        steps_per_row = pages_per_row // n_pages
        assert n_i * n_k == bsz * steps_per_row, "one group of pages per grid step"

        def page_spec(r):
            def index(i, k, pt):
                step = i * n_k + k
                return (0, pt[step // steps_per_row, (step % steps_per_row) * n_pages + r], 0, 0, 0)
            return pl.BlockSpec((1, 1, page, heads, hd), index)

        def sums_index(i, k, pt):
            step = i * n_k + k
            return (step // steps_per_row, step % steps_per_row, 0, 0)

        in_specs += [page_spec(r) for r in range(n_pages)]
        operands = [page_table] + operands + [cache] * n_pages
        out_shape.append(jax.ShapeDtypeStruct((bsz, pages_per_row // per_blk, heads, hd), F32))
        out_specs.append(pl.BlockSpec((1, n_pages // per_blk, heads, hd), sums_index))
    out = pl.pallas_call(
        functools.partial(_ffn_kernel, alpha=alpha, n_pages=n_pages, per_blk=per_blk),
        out_shape=out_shape,
        grid_spec=pltpu.PrefetchScalarGridSpec(
            num_scalar_prefetch=1 if n_pages else 0,
            grid=(n_i, n_k),
            in_specs=in_specs,
            out_specs=out_specs,
            scratch_shapes=[pltpu.VMEM((tm, d), BF16), pltpu.VMEM((tm, d), F32)],
        ),
        compiler_params=_params("arbitrary", "arbitrary"),
        name="ffn_ln2",
    )(*operands)
    return out if n_pages else out[0]


def _select_kernel(bs_ref, q_ref, kn_ref, idx_ref, *, blk, topk):
    n_past = bs_ref.shape[1]
    q = q_ref[0]
    past = jnp.sum((bs_ref[0] / blk) * q, axis=-1, keepdims=True)
    own = jnp.sum((kn_ref[0] / blk) * q, axis=-1, keepdims=True)[None]
    gate = jnp.concatenate([past, own], axis=0)
    n = lax.broadcasted_iota(jnp.int32, gate.shape, 0)
    gate = jnp.where(n < n_past, gate, NEG)
    picks = []
    for _ in range(topk):
        best = jnp.max(gate, axis=0, keepdims=True)
        first = jnp.min(jnp.where(gate == best, n, n_past + 1), axis=0, keepdims=True)
        picks.append(first)
        gate = jnp.where(n == first, -jnp.inf, gate)
    idx = jnp.concatenate(picks, axis=0)
    idx_ref[0] = jnp.broadcast_to(idx, idx_ref.shape[1:])


def _select_blocks(bsum, q, k_new):
    bsz, n_past, heads, hd = bsum.shape
    kern = functools.partial(_select_kernel, blk=MOBA_BLOCK, topk=MOBA_TOPK)
    tok = pl.BlockSpec((1, heads, hd), lambda b: (b, 0, 0))
    return pl.pallas_call(
        kern,
        out_shape=jax.ShapeDtypeStruct((bsz, MOBA_TOPK, heads, V7X_LANES), jnp.int32),
        grid=(bsz,),
        in_specs=[pl.BlockSpec((1, n_past, heads, hd), lambda b: (b, 0, 0, 0)), tok, tok],
        out_specs=pl.BlockSpec((1, MOBA_TOPK, heads, V7X_LANES), lambda b: (b, 0, 0, 0)),
        compiler_params=_params("arbitrary"),
        name="moba_select",
    )(bsum, q, k_new)


def _moba_sample_kernel(pt_ref, idx_ref, q_ref, kn_ref, vn_ref, *refs):
    del pt_ref, idx_ref
    o_ref = refs[-1]
    n_sel = (len(refs) - 1) // 2
    hd = q_ref.shape[-1]
    q = q_ref[0, 0] * (hd ** -0.5)
    ks = jnp.concatenate([r[0, :, 0, 0, :] for r in refs[:n_sel]], axis=0)
    vs = jnp.concatenate([r[0, :, 0, 0, :] for r in refs[n_sel:2 * n_sel]], axis=0)
    q8 = jnp.broadcast_to(q, (8, hd)).astype(BF16)
    s_sel = lax.dot_general(q8, ks.astype(BF16), NT, preferred_element_type=F32)
    s_own = jnp.sum(q * kn_ref[0, 0], axis=-1, keepdims=True)
    mx = jnp.maximum(jnp.max(s_sel, axis=-1, keepdims=True), s_own)
    p_sel = jnp.exp(s_sel - mx)
    p_own = jnp.exp(s_own - mx)
    den = jnp.sum(p_sel, axis=-1, keepdims=True) + p_own
    o = jnp.dot(p_sel.astype(BF16), vs.astype(BF16), preferred_element_type=F32)
    o = (o + p_own * vn_ref[0, 0]) / den
    o_ref[0, 0] = o[0:1]


def _moba_sample(cache_k, cache_v, page_table, idx, q, k_new, v_new, *, heads, hd):
    bsz = q.shape[0]
    per_blk = MOBA_BLOCK // PAGE_ROWS

    def cache_spec(j, r):
        def index(b, h, pt, ix):
            blk = ix[(b * heads + h) * MOBA_TOPK + j]
            return (pt[b, blk * per_blk + r], 0, h, 0, 0)
        return pl.BlockSpec((1, PAGE_ROWS, 1, 1, hd), index)

    tok = pl.BlockSpec((1, 1, 1, hd), lambda b, h, pt, ix: (b, h, 0, 0))
    sel_specs = [cache_spec(j, r) for j in range(MOBA_TOPK) for r in range(per_blk)]
    shape4 = (bsz, heads, 1, hd)
    out = pl.pallas_call(
        _moba_sample_kernel,
        out_shape=jax.ShapeDtypeStruct(shape4, F32),
        grid_spec=pltpu.PrefetchScalarGridSpec(
            num_scalar_prefetch=2,
            grid=(bsz, heads),
            in_specs=[tok, tok, tok] + sel_specs + sel_specs,
            out_specs=tok,
        ),
        compiler_params=_params("arbitrary", "arbitrary"),
        name="moba_sample",
    )(page_table, idx, q.reshape(shape4), k_new.reshape(shape4), v_new.reshape(shape4),
      *([cache_k] * len(sel_specs)), *([cache_v] * len(sel_specs)))
    return out.reshape(bsz, heads * hd)


def _mem_sample_kernel(q_ref, k_ref, v_ref, o_ref):
    hd = q_ref.shape[-1]
    q = q_ref[0] * (hd ** -0.5)
    s = jnp.sum(k_ref[0, 0] * q, axis=-1, keepdims=True)
    mx = jnp.max(s, axis=0, keepdims=True)
    p = jnp.exp(s - mx)
    den = jnp.sum(p, axis=0)
    o = jnp.sum(p * v_ref[0, 0], axis=0)
    o_ref[0] = o / den


def _mem_sample(q, mem_k, mem_v):
    _, bsz, n_mem, heads, hd = mem_k.shape
    tok = pl.BlockSpec((1, heads, hd), lambda b: (b, 0, 0))
    mem = pl.BlockSpec((1, 1, n_mem, heads, hd), lambda b: (0, b, 0, 0, 0))
    return pl.pallas_call(
        _mem_sample_kernel,
        out_shape=jax.ShapeDtypeStruct((bsz, heads, hd), F32),
        grid=(bsz,),
        in_specs=[tok, mem, mem],
        out_specs=tok,
        compiler_params=_params("arbitrary"),
        name="mem_sample",
    )(q, mem_k, mem_v)


def _pool_sample_kernel(u_ref, st_ref, w_ref, sc_ref, o_ref, *, windows):
    ctx = st_ref.shape[0]
    gc = w_ref.shape[-1]
    for g, w in enumerate(windows):
        sl = slice(g * gc, (g + 1) * gc)
        x = u_ref[:, sl]
        win = x
        for j in range(1, w):
            win = win + st_ref[ctx - j, :, sl]
        mix = win / float(w) - x
        y = jnp.dot(mix.astype(BF16), w_ref[g], preferred_element_type=F32)
        o_ref[:, sl] = (y * sc_ref[:, sl]).astype(o_ref.dtype)


def _pool_sample(u, state, w_pool, scale):
    bsz, width = u.shape
    return pl.pallas_call(
        functools.partial(_pool_sample_kernel, windows=POOL_WINDOWS),
        out_shape=jax.ShapeDtypeStruct((bsz, width), BF16),
        name="pool_sample",
        compiler_params=pltpu.CompilerParams(vmem_limit_bytes=V7X_VMEM_LIMIT_BYTES),
    )(u, state, w_pool, scale)


def kernel(x_prompt, x_sample, cache_k, cache_v, cache_mem_k, cache_mem_v, state_pool, page_table, mem_prompt, w_in, w_mem_kv, w_pool, pool_scale, w_br_attn, w_br_pool, w_br_mem, w_o, ln1_g, ln1_b, w_up, w_down, ln2_g, ln2_b):
    depth = w_in.shape[0]
    assert depth == 1, "single-layer trunk"
    batch, seq, d_model = x_prompt.shape
    dec_batch, dec_seq, _ = x_sample.shape
    assert dec_seq == 1
    _, n_phys, page, a_heads, a_hd = cache_k.shape
    assert page == PAGE_ROWS
    _, _, n_mem, m_heads, m_hd = cache_mem_k.shape
    pool_ctx, pool_width = state_pool.shape[2], state_pool.shape[3]
    a_width = a_heads * a_hd
    m_width = m_heads * m_hd
    alpha = float((2 * depth) ** 0.25)
    assert a_width == pool_width == m_width == COL_TILE
    k_tile, v_tile = 1, 2
    n_tiles = w_in.shape[2] // COL_TILE

    w_o_b = w_o[0].astype(BF16)
    w_up_b = w_up[0].astype(BF16)
    w_down_b = w_down[0].astype(BF16)
    w_pool_b = w_pool[0].astype(BF16)

    mp = batch * seq
    xp = x_prompt.reshape(mp, d_model)
    u_col, mq_col, gate_col = 1, 2, 3
    p3 = _project(xp, w_in[0], lambda j: jnp.where(j == 0, 0, j + 2), n_tiles - 2,
                  tm=1024, name="in_proj_prompt")[0]
    k3, k_heads = _project(xp, w_in[0], lambda j: k_tile, 1, tm=1024, name="k_proj_prompt", heads=a_heads)
    v3, v_heads = _project(xp, w_in[0], lambda j: v_tile, 1, tm=1024, name="v_proj_prompt", heads=a_heads)
    attn_p = _moba_prompt(p3, k3[0], v3[0], batch=batch, seq=seq, heads=a_heads, hd=a_hd, q_col=0)
    pool_p = _pool_prompt(p3, w_pool_b, pool_scale, batch=batch, seq=seq, u_col=u_col)
    kv3 = _project(mem_prompt.reshape(batch * n_mem, d_model), w_mem_kv[0], lambda j: j, 2,
                   tm=batch * n_mem, name="mem_kv_prompt")[0]
    memo_p = _mem_prompt(p3, kv3, batch=batch, seq=seq, n_mem=n_mem, heads=m_heads, hd=m_hd, q_col=mq_col)
    mix_p = _merge(attn_p, pool_p, memo_p, w_br_attn[0], w_br_pool[0], w_br_mem[0], p3,
                   gate_col=gate_col, tm=512)
    h_p = _oproj(xp, mix_p, w_o_b, ln1_g, ln1_b, alpha=alpha, tm=512)
    y_p, bsum = _ffn(h_p, w_up_b, w_down_b, ln2_g, ln2_b, alpha=alpha, tm=512, tk=512,
                     cache=cache_k, page_table=page_table)

    xs = x_sample.reshape(dec_batch, d_model)
    p3s = _project(xs, w_in[0], lambda j: j, n_tiles, tm=dec_batch, name="in_proj_sample")[0]
    q_s, k_s, v_s, u_s, mq_s = p3s[0], p3s[k_tile], p3s[v_tile], p3s[3], p3s[4]
    head_shape = (dec_batch, a_heads, a_hd)
    picks = _select_blocks(bsum, q_s.reshape(head_shape), k_s.reshape(head_shape))
    idx = jnp.transpose(picks[:, :, :, 0], (0, 2, 1)).reshape(-1)
    row_tiled = (n_phys, page, a_heads, 1, a_hd)
    attn_s = _moba_sample(cache_k.reshape(row_tiled), cache_v.reshape(row_tiled), page_table, idx,
                          q_s, k_s, v_s, heads=a_heads, hd=a_hd)
    memo_s = _mem_sample(mq_s.reshape(dec_batch, m_heads, m_hd), cache_mem_k, cache_mem_v)
    pool_s = _pool_sample(u_s, jnp.transpose(state_pool[0], (1, 0, 2)), w_pool_b, pool_scale)
    mix_s = _merge(attn_s, pool_s, memo_s.reshape(dec_batch, m_width), w_br_attn[0], w_br_pool[0], w_br_mem[0],
                   p3s, gate_col=5, tm=dec_batch)
    h_s = _oproj(xs, mix_s, w_o_b, ln1_g, ln1_b, alpha=alpha, tm=dec_batch)
    y_s = _ffn(h_s, w_up_b, w_down_b, ln2_g, ln2_b, alpha=alpha, tm=dec_batch, tk=2048)

    kv_shape = (depth, batch, seq, a_heads, a_hd)
    mem_shape = (depth, batch, n_mem, m_heads, m_hd)
    new_pool_p = p3.reshape(-1, batch, seq, pool_width)[u_col, :, seq - pool_ctx:, :][None]
    new_pool_s = jnp.concatenate([state_pool[0][:, 1:, :], u_s[:, None, :]], axis=1)[None]
    skv_shape = (depth, dec_batch, dec_seq, a_heads, a_hd)
    return (y_p.reshape(batch, seq, d_model), y_s.reshape(dec_batch, dec_seq, d_model),
            k_heads.reshape(kv_shape), v_heads.reshape(kv_shape),
            kv3[0].reshape(mem_shape), kv3[1].reshape(mem_shape),
            new_pool_p,
            k_s.reshape(skv_shape), v_s.reshape(skv_shape),
            new_pool_s)
```

```python
import functools

import jax
import jax.numpy as jnp
from jax import lax
from jax.experimental import pallas as pl
from jax.experimental.pallas import tpu as pltpu
from jax.experimental.pallas import tpu_sc as plsc

MOBA_BLOCK = 256
MOBA_TOPK = 3
POOL_WINDOWS = (2, 4, 8, 16)
N_BRANCH = 3
LN_EPS = 1e-5
NEG = -1e30
PAGE_ROWS = 128

V7X_VMEM_LIMIT_BYTES = 60000 * 1024
V7X_LANES = 128
V7X_SC_LANES = 16
COL_TILE = 1024
SC_CHUNK_ROWS = 256

BF16 = jnp.bfloat16
F32 = jnp.float32
NT = (((1,), (1,)), ((), ()))


def _params(*semantics):
    return pltpu.CompilerParams(dimension_semantics=semantics,
                                vmem_limit_bytes=V7X_VMEM_LIMIT_BYTES)


def _layer_norm(x, g, b):
    mu = jnp.mean(x, axis=-1, keepdims=True)
    xc = x - mu
    var = jnp.mean(xc * xc, axis=-1, keepdims=True)
    return xc * lax.rsqrt(var + LN_EPS) * g + b


def _proj_kernel(x_ref, w_ref, o_ref, *rest, heads):
    wb_ref = rest[-1]

    @pl.when(pl.program_id(1) == 0)
    def _():
        wb_ref[...] = w_ref[...].astype(BF16)

    y = jnp.dot(x_ref[...].astype(BF16), wb_ref[...], preferred_element_type=F32)
    o_ref[0] = y
    if heads:
        oh_ref = rest[0]
        tm, hd = y.shape[0], y.shape[1] // heads
        for h in range(heads):
            oh_ref[pl.ds(h, tm, stride=heads), :] = y[:, h * hd:(h + 1) * hd]


def _project(x, w, col_of, n_cols, *, tm, name, heads=0):
    m, k = x.shape
    tn = COL_TILE
    out_shape = [jax.ShapeDtypeStruct((n_cols, m, tn), F32)]
    out_specs = [pl.BlockSpec((1, tm, tn), lambda j, i: (j, i, 0))]
    if heads:
        assert n_cols == 1
        out_shape.append(jax.ShapeDtypeStruct((m * heads, tn // heads), F32))
        out_specs.append(pl.BlockSpec((tm * heads, tn // heads), lambda j, i: (i, 0)))
    return pl.pallas_call(
        functools.partial(_proj_kernel, heads=heads),
        out_shape=out_shape,
        grid=(n_cols, m // tm),
        in_specs=[pl.BlockSpec((tm, k), lambda j, i: (i, 0)),
                  pl.BlockSpec((k, tn), lambda j, i: (0, col_of(j)))],
        out_specs=out_specs,
        scratch_shapes=[pltpu.VMEM((k, tn), BF16)],
        compiler_params=_params("arbitrary", "arbitrary"),
        name=name,
    )(x, w)


def _moba_prompt_kernel(q_ref, k_ref, v_ref, o_ref, *, seq, blk, topk):
    nb = seq // blk
    hd = q_ref.shape[-1]
    q = q_ref[0]
    k = k_ref[...]
    means = jnp.concatenate(
        [jnp.mean(k[n * blk:(n + 1) * blk], axis=0, keepdims=True) for n in range(nb)], axis=0)
    qs = q * (hd ** -0.5)
    kb = k.astype(BF16)
    vb = v_ref[...].astype(BF16)
    key_blk = lax.broadcasted_iota(jnp.int32, (seq, hd), 0) // blk
    lane = lax.broadcasted_iota(jnp.int32, (seq, hd), 1)
    k_ext = jnp.concatenate([kb, jnp.where(key_blk == lane, 1.0, 0.0).astype(BF16)], axis=1)
    row = lax.broadcasted_iota(jnp.int32, (blk, blk), 0)
    col = lax.broadcasted_iota(jnp.int32, (blk, blk), 1)
    causal = col <= row
    blk_id = lax.broadcasted_iota(jnp.int32, (nb, blk), 0)
    for i in range(nb):
        qi = qs[i * blk:(i + 1) * blk]
        if i > topk:
            gate = lax.dot_general(means, q[i * blk:(i + 1) * blk], NT,
                                   precision=lax.Precision.HIGHEST, preferred_element_type=F32)
            beaten = jnp.zeros((nb, blk), jnp.int32)
            for m in range(i):
                gm = gate[m:m + 1, :]
                beaten = beaten + jnp.where(m < blk_id, jnp.where(gm >= gate, 1, 0), jnp.where(gm > gate, 1, 0))
            bias = jnp.where((beaten < topk) | (blk_id >= i), 0.0, NEG)
            bias = jnp.concatenate([bias, jnp.zeros((hd - nb, blk), F32)], axis=0).T
            q_ext = jnp.concatenate([qi.astype(BF16), bias.astype(BF16)], axis=1)
            s = lax.dot_general(q_ext, k_ext[:(i + 1) * blk], NT, preferred_element_type=F32)
        else:
            s = lax.dot_general(qi.astype(BF16), kb[:(i + 1) * blk], NT, preferred_element_type=F32)
        own = jnp.where(causal, s[:, i * blk:(i + 1) * blk], NEG)
        s = jnp.concatenate([s[:, :i * blk], own], axis=1) if i else own
        mx = jnp.max(s, axis=-1, keepdims=True)
        p = jnp.exp(s - mx)
        den = jnp.sum(p, axis=-1, keepdims=True)
        o = jnp.dot(p.astype(BF16), vb[:(i + 1) * blk], preferred_element_type=F32)
        o_ref[i * blk:(i + 1) * blk, :] = (o / den).astype(o_ref.dtype)


def _moba_prompt(p3, k2, v2, *, batch, seq, heads, hd, q_col):
    assert heads * hd == COL_TILE
    kern = functools.partial(_moba_prompt_kernel, seq=seq, blk=MOBA_BLOCK, topk=MOBA_TOPK)
    kv = pl.BlockSpec((seq, hd), lambda b, h: (b, h))
    return pl.pallas_call(
        kern,
        out_shape=jax.ShapeDtypeStruct((batch * seq, heads * hd), BF16),
        grid=(batch, heads),
        in_specs=[pl.BlockSpec((1, seq, hd), lambda b, h: (q_col, b, h)), kv, kv],
        out_specs=kv,
        compiler_params=_params("arbitrary", "arbitrary"),
        name="moba_prompt",
    )(p3, k2, v2)


def _pool_prompt_kernel(u_ref, w_ref, sc_ref, o_ref, *, windows):
    seq = u_ref.shape[1]
    gc = w_ref.shape[-1]
    t = lax.broadcasted_iota(jnp.int32, (seq, 1), 0)
    for g, w in enumerate(windows):
        x = u_ref[0, :, g * gc:(g + 1) * gc]
        win = x
        width = 1
        while width < w:
            shifted = pltpu.roll(win, width, 0)
            win = win + jnp.where(t >= width, shifted, 0.0)
            width *= 2
        cnt = jnp.minimum(w, t + 1).astype(F32)
        mix = win / cnt - x
        y = jnp.dot(mix.astype(BF16), w_ref[g], preferred_element_type=F32)
        o_ref[:, g * gc:(g + 1) * gc] = (y * sc_ref[:, g * gc:(g + 1) * gc]).astype(o_ref.dtype)


def _pool_prompt(p3, w_pool, scale, *, batch, seq, u_col):
    groups, gc, _ = w_pool.shape
    width = groups * gc
    assert width == COL_TILE
    kern = functools.partial(_pool_prompt_kernel, windows=POOL_WINDOWS)
    return pl.pallas_call(
        kern,
        out_shape=jax.ShapeDtypeStruct((batch * seq, width), BF16),
        grid=(batch,),
        in_specs=[pl.BlockSpec((1, seq, width), lambda b: (u_col, b, 0)),
                  pl.BlockSpec((groups, gc, gc), lambda b: (0, 0, 0)),
                  pl.BlockSpec((1, width), lambda b: (0, 0))],
        out_specs=pl.BlockSpec((seq, width), lambda b: (b, 0)),
        compiler_params=_params("arbitrary"),
        name="pool_prompt",
    )(p3, w_pool, scale)


def _mem_prompt_kernel(q_ref, k_ref, v_ref, o_ref):
    hd = q_ref.shape[-1]
    qb = (q_ref[0] * (hd ** -0.5)).astype(BF16)
    s = lax.dot_general(qb, k_ref[0].astype(BF16), NT, preferred_element_type=F32)
    mx = jnp.max(s, axis=-1, keepdims=True)
    p = jnp.exp(s - mx)
    den = jnp.sum(p, axis=-1, keepdims=True)
    o = jnp.dot(p.astype(BF16), v_ref[0].astype(BF16), preferred_element_type=F32)
    o_ref[...] = (o / den).astype(o_ref.dtype)


def _mem_prompt(p3, kv3, *, batch, seq, n_mem, heads, hd, q_col):
    assert heads * hd == COL_TILE
    return pl.pallas_call(
        _mem_prompt_kernel,
        out_shape=jax.ShapeDtypeStruct((batch * seq, heads * hd), BF16),
        grid=(batch, heads),
        in_specs=[pl.BlockSpec((1, seq, hd), lambda b, h: (q_col, b, h)),
                  pl.BlockSpec((1, n_mem, hd), lambda b, h: (0, b, h)),
                  pl.BlockSpec((1, n_mem, hd), lambda b, h: (1, b, h))],
        out_specs=pl.BlockSpec((seq, hd), lambda b, h: (b, h)),
        compiler_params=_params("arbitrary", "arbitrary"),
        name="mem_prompt",
    )(p3, kv3, kv3)


def _merge_kernel(a_ref, p_ref, m_ref, wa_ref, wp_ref, wm_ref, ga_ref, gp_ref, gm_ref, o_ref,
                  wab_ref, wpb_ref, wmb_ref):
    @pl.when(pl.program_id(1) == 0)
    def _():
        wab_ref[...] = wa_ref[...].astype(BF16)
        wpb_ref[...] = wp_ref[...].astype(BF16)
        wmb_ref[...] = wm_ref[...].astype(BF16)

    def branch(x_ref, w_ref, g_ref):
        y = jnp.dot(x_ref[...].astype(BF16), w_ref[...], preferred_element_type=F32)
        return jax.nn.sigmoid(g_ref[0]) * y
    mix = branch(a_ref, wab_ref, ga_ref) + branch(p_ref, wpb_ref, gp_ref) + branch(m_ref, wmb_ref, gm_ref)
    o_ref[...] = mix.astype(o_ref.dtype)


def _merge(attn, pool, memo, w_ba, w_bp, w_bm, p3, *, gate_col, tm):
    m, kw = attn.shape
    d = w_ba.shape[1]
    tn = COL_TILE
    per_gate = d // tn
    lhs = pl.BlockSpec((tm, kw), lambda j, i: (i, 0))
    wsp = pl.BlockSpec((kw, tn), lambda j, i: (0, j))
    gsp = lambda c: pl.BlockSpec((1, tm, tn), lambda j, i: (gate_col + c * per_gate + j, i, 0))
    return pl.pallas_call(
        _merge_kernel,
        out_shape=jax.ShapeDtypeStruct((m, d), BF16),
        grid=(d // tn, m // tm),
        in_specs=[lhs, lhs, lhs, wsp, wsp, wsp, gsp(0), gsp(1), gsp(2)],
        out_specs=pl.BlockSpec((tm, tn), lambda j, i: (i, j)),
        scratch_shapes=[pltpu.VMEM((kw, tn), BF16)] * N_BRANCH,
        compiler_params=_params("arbitrary", "arbitrary"),
        name="merge",
    )(attn, pool, memo, w_ba, w_bp, w_bm, p3, p3, p3)


def _oproj_kernel(x_ref, mix_ref, w_ref, g_ref, b_ref, o_ref, *, alpha):
    y = jnp.dot(mix_ref[...], w_ref[...], preferred_element_type=F32)
    o_ref[...] = _layer_norm(alpha * x_ref[...] + y, g_ref[...], b_ref[...])


def _oproj(x, mix, w_o, g, b, *, alpha, tm):
    m, d = x.shape
    row = pl.BlockSpec((tm, d), lambda i: (i, 0))
    vec = pl.BlockSpec((1, d), lambda i: (0, 0))
    return pl.pallas_call(
        functools.partial(_oproj_kernel, alpha=alpha),
        out_shape=jax.ShapeDtypeStruct((m, d), F32),
        grid=(m // tm,),
        in_specs=[row, row, pl.BlockSpec((d, d), lambda i: (0, 0)), vec, vec],
        out_specs=row,
        compiler_params=_params("arbitrary"),
        name="oproj_ln1",
    )(x, mix, w_o, g, b)


def _ffn_kernel(h_ref, wu_ref, wd_ref, g_ref, b_ref, o_ref, hb_ref, acc_ref, *, alpha):
    kk = pl.program_id(1)

    @pl.when(kk == 0)
    def _():
        hb_ref[...] = h_ref[...].astype(BF16)
        acc_ref[...] = jnp.zeros_like(acc_ref)

    a = jnp.dot(hb_ref[...], wu_ref[...], preferred_element_type=F32)
    a = jnp.square(jnp.maximum(a, 0.0)).astype(BF16)
    acc_ref[...] += jnp.dot(a, wd_ref[...], preferred_element_type=F32)

    @pl.when(kk == pl.num_programs(1) - 1)
    def _():
        o_ref[...] = _layer_norm(alpha * h_ref[...] + acc_ref[...], g_ref[...], b_ref[...])


def _ffn(h, w_up, w_down, g, b, *, alpha, tm, tk):
    m, d = h.shape
    dff = w_up.shape[1]
    row = pl.BlockSpec((tm, d), lambda i, k: (i, 0))
    vec = pl.BlockSpec((1, d), lambda i, k: (0, 0))
    return pl.pallas_call(
        functools.partial(_ffn_kernel, alpha=alpha),
        out_shape=jax.ShapeDtypeStruct((m, d), F32),
        grid=(m // tm, dff // tk),
        in_specs=[row,
                  pl.BlockSpec((d, tk), lambda i, k: (0, k)),
                  pl.BlockSpec((tk, d), lambda i, k: (k, 0)),
                  vec, vec],
        out_specs=row,
        scratch_shapes=[pltpu.VMEM((tm, d), BF16), pltpu.VMEM((tm, d), F32)],
        compiler_params=_params("arbitrary", "arbitrary"),
        name="ffn_ln2",
    )(h, w_up, w_down, g, b)


def _block_sums(cache, page_table):
    _, n_phys, page, heads, hd = cache.shape
    bsz, n_pages = page_table.shape
    lanes = V7X_SC_LANES
    per_blk = MOBA_BLOCK // page
    n_blk = n_pages // per_blk
    rows_per_page = page * heads
    n_ch = rows_per_page // SC_CHUNK_ROWS
    sub = SC_CHUNK_ROWS // lanes
    assert sub % heads == 0 and (n_pages * n_ch * lanes) % V7X_LANES == 0
    items_per_blk = per_blk * n_ch
    n_items = n_pages * n_ch
    pieces = cache.reshape(n_phys * n_ch * lanes, sub, hd)
    chunk_id = page_table[:, :, None] * n_ch + jnp.arange(n_ch, dtype=jnp.int32)
    piece_id = chunk_id.reshape(bsz, n_items, 1) * lanes + jnp.arange(lanes, dtype=jnp.int32)
    piece_id = piece_id.reshape(bsz, n_items * lanes // V7X_LANES, V7X_LANES)
    ids_per_row = V7X_LANES // lanes
    mesh = plsc.VectorSubcoreMesh(core_axis_name="core", subcore_axis_name="subcore")
    assert bsz == mesh.num_cores * mesh.num_subcores, "one batch row per vector subcore"

    def kern(id_hbm, c_hbm, o_hbm, buf, acc, id_v, sems):
        b = lax.axis_index("core") * mesh.num_subcores + lax.axis_index("subcore")
        pltpu.sync_copy(id_hbm.at[b], id_v)

        def chunk_copy(item, slot):
            ids = id_v[item // ids_per_row, pl.ds((item % ids_per_row) * lanes, lanes)]
            return pltpu.make_async_copy(c_hbm.at[ids], buf.at[slot], sems.at[slot])

        chunk_copy(0, 0).start()

        @pl.loop(0, n_blk)
        def _(n):
            for h in range(heads):
                for c in range(hd // lanes):
                    acc[h, pl.ds(c * lanes, lanes)] = jnp.zeros((lanes,), F32)
            for j in range(items_per_blk):
                slot = j % 2
                item = n * items_per_blk + j
                chunk_copy(item, slot).wait()

                @pl.when(item + 1 < n_items)
                def _():
                    chunk_copy(item + 1, 1 - slot).start()

                @pl.loop(0, heads * (hd // lanes))
                def _(hc):
                    h = hc // (hd // lanes)
                    cols = pl.ds((hc % (hd // lanes)) * lanes, lanes)
                    rows = [(p, t * heads + h) for p in range(lanes) for t in range(sub // heads)]
                    parts = [buf[slot, p, r, cols] for p, r in rows[:4]]
                    for i, (p, r) in enumerate(rows[4:]):
                        parts[i % 4] = parts[i % 4] + buf[slot, p, r, cols]
                    plsc.addupdate(acc.at[h, cols], (parts[0] + parts[1]) + (parts[2] + parts[3]))
            pltpu.sync_copy(acc, o_hbm.at[pl.ds((b * n_blk + n) * heads, heads)])

    out = pl.kernel(
        kern,
        out_type=jax.ShapeDtypeStruct((bsz * n_blk * heads, hd), F32),
        mesh=mesh,
        scratch_types=[pltpu.VMEM((2, lanes, sub, hd), F32),
                       pltpu.VMEM((heads, hd), F32),
                       pltpu.VMEM((n_items * lanes // V7X_LANES, V7X_LANES), jnp.int32),
                       pltpu.SemaphoreType.DMA((2,))],
        name="block_sums",
    )(piece_id, pieces)
    return out.reshape(bsz, n_blk, heads, hd)


def _select_kernel(bs_ref, q_ref, kn_ref, idx_ref, *, blk, topk):
    n_past = bs_ref.shape[1]
    q = q_ref[0]
    past = jnp.sum((bs_ref[0] / blk) * q, axis=-1, keepdims=True)
    own = jnp.sum((kn_ref[0] / blk) * q, axis=-1, keepdims=True)[None]
    gate = jnp.concatenate([past, own], axis=0)
    n = lax.broadcasted_iota(jnp.int32, gate.shape, 0)
    gate = jnp.where(n < n_past, gate, NEG)
    picks = []
    for _ in range(topk):
        best = jnp.max(gate, axis=0, keepdims=True)
        first = jnp.min(jnp.where(gate == best, n, n_past + 1), axis=0, keepdims=True)
        picks.append(first)
        gate = jnp.where(n == first, -jnp.inf, gate)
    idx = jnp.concatenate(picks, axis=0)
    idx_ref[0] = jnp.broadcast_to(idx, idx_ref.shape[1:])


def _select_blocks(bsum, q, k_new):
    bsz, n_past, heads, hd = bsum.shape
    kern = functools.partial(_select_kernel, blk=MOBA_BLOCK, topk=MOBA_TOPK)
    tok = pl.BlockSpec((1, heads, hd), lambda b: (b, 0, 0))
    return pl.pallas_call(
        kern,
        out_shape=jax.ShapeDtypeStruct((bsz, MOBA_TOPK, heads, V7X_LANES), jnp.int32),
        grid=(bsz,),
        in_specs=[pl.BlockSpec((1, n_past, heads, hd), lambda b: (b, 0, 0, 0)), tok, tok],
        out_specs=pl.BlockSpec((1, MOBA_TOPK, heads, V7X_LANES), lambda b: (b, 0, 0, 0)),
        compiler_params=_params("arbitrary"),
        name="moba_select",
    )(bsum, q, k_new)


def _moba_sample_kernel(pt_ref, idx_ref, q_ref, kn_ref, vn_ref, *refs):
    del pt_ref, idx_ref
    o_ref = refs[-1]
    n_sel = (len(refs) - 1) // 2
    hd = q_ref.shape[-1]
    q = q_ref[0, 0] * (hd ** -0.5)
    ks = jnp.concatenate([r[0, :, 0, 0, :] for r in refs[:n_sel]], axis=0)
    vs = jnp.concatenate([r[0, :, 0, 0, :] for r in refs[n_sel:2 * n_sel]], axis=0)
    q8 = jnp.broadcast_to(q, (8, hd)).astype(BF16)
    s_sel = lax.dot_general(q8, ks.astype(BF16), NT, preferred_element_type=F32)
    s_own = jnp.sum(q * kn_ref[0, 0], axis=-1, keepdims=True)
    mx = jnp.maximum(jnp.max(s_sel, axis=-1, keepdims=True), s_own)
    p_sel = jnp.exp(s_sel - mx)
    p_own = jnp.exp(s_own - mx)
    den = jnp.sum(p_sel, axis=-1, keepdims=True) + p_own
    o = jnp.dot(p_sel.astype(BF16), vs.astype(BF16), preferred_element_type=F32)
    o = (o + p_own * vn_ref[0, 0]) / den
    o_ref[0, 0] = o[0:1]


def _moba_sample(cache_k, cache_v, page_table, idx, q, k_new, v_new, *, heads, hd):
    bsz = q.shape[0]
    per_blk = MOBA_BLOCK // PAGE_ROWS

    def cache_spec(j, r):
        def index(b, h, pt, ix):
            blk = ix[(b * heads + h) * MOBA_TOPK + j]
            return (pt[b, blk * per_blk + r], 0, h, 0, 0)
        return pl.BlockSpec((1, PAGE_ROWS, 1, 1, hd), index)

    tok = pl.BlockSpec((1, 1, 1, hd), lambda b, h, pt, ix: (b, h, 0, 0))
    sel_specs = [cache_spec(j, r) for j in range(MOBA_TOPK) for r in range(per_blk)]
    shape4 = (bsz, heads, 1, hd)
    out = pl.pallas_call(
        _moba_sample_kernel,
        out_shape=jax.ShapeDtypeStruct(shape4, F32),
        grid_spec=pltpu.PrefetchScalarGridSpec(
            num_scalar_prefetch=2,
            grid=(bsz, heads),
            in_specs=[tok, tok, tok] + sel_specs + sel_specs,
            out_specs=tok,
        ),
        compiler_params=_params("arbitrary", "arbitrary"),
        name="moba_sample",
    )(page_table, idx, q.reshape(shape4), k_new.reshape(shape4), v_new.reshape(shape4),
      *([cache_k] * len(sel_specs)), *([cache_v] * len(sel_specs)))
    return out.reshape(bsz, heads * hd)


def _mem_sample_kernel(q_ref, k_ref, v_ref, o_ref):
    hd = q_ref.shape[-1]
    q = q_ref[0] * (hd ** -0.5)
    s = jnp.sum(k_ref[0, 0] * q, axis=-1, keepdims=True)
    mx = jnp.max(s, axis=0, keepdims=True)
    p = jnp.exp(s - mx)
    den = jnp.sum(p, axis=0)
    o = jnp.sum(p * v_ref[0, 0], axis=0)
    o_ref[0] = o / den


def _mem_sample(q, mem_k, mem_v):
    _, bsz, n_mem, heads, hd = mem_k.shape
    tok = pl.BlockSpec((1, heads, hd), lambda b: (b, 0, 0))
    mem = pl.BlockSpec((1, 1, n_mem, heads, hd), lambda b: (0, b, 0, 0, 0))
    return pl.pallas_call(
        _mem_sample_kernel,
        out_shape=jax.ShapeDtypeStruct((bsz, heads, hd), F32),
        grid=(bsz,),
        in_specs=[tok, mem, mem],
        out_specs=tok,
        compiler_params=_params("arbitrary"),
        name="mem_sample",
    )(q, mem_k, mem_v)


def _pool_sample_kernel(u_ref, st_ref, w_ref, sc_ref, o_ref, *, windows):
    ctx = st_ref.shape[0]
    gc = w_ref.shape[-1]
    for g, w in enumerate(windows):
        sl = slice(g * gc, (g + 1) * gc)
        x = u_ref[:, sl]
        win = x
        for j in range(1, w):
            win = win + st_ref[ctx - j, :, sl]
        mix = win / float(w) - x
        y = jnp.dot(mix.astype(BF16), w_ref[g], preferred_element_type=F32)
        o_ref[:, sl] = (y * sc_ref[:, sl]).astype(o_ref.dtype)


def _pool_sample(u, state, w_pool, scale):
    bsz, width = u.shape
    return pl.pallas_call(
        functools.partial(_pool_sample_kernel, windows=POOL_WINDOWS),
        out_shape=jax.ShapeDtypeStruct((bsz, width), BF16),
        name="pool_sample",
        compiler_params=pltpu.CompilerParams(vmem_limit_bytes=V7X_VMEM_LIMIT_BYTES),
    )(u, state, w_pool, scale)


def kernel(x_prompt, x_sample, cache_k, cache_v, cache_mem_k, cache_mem_v, state_pool, page_table, mem_prompt, w_in, w_mem_kv, w_pool, pool_scale, w_br_attn, w_br_pool, w_br_mem, w_o, ln1_g, ln1_b, w_up, w_down, ln2_g, ln2_b):
    depth = w_in.shape[0]
    assert depth == 1, "single-layer trunk"
    batch, seq, d_model = x_prompt.shape
    dec_batch, dec_seq, _ = x_sample.shape
    assert dec_seq == 1
    _, n_phys, page, a_heads, a_hd = cache_k.shape
    assert page == PAGE_ROWS
    _, _, n_mem, m_heads, m_hd = cache_mem_k.shape
    pool_ctx, pool_width = state_pool.shape[2], state_pool.shape[3]
    a_width = a_heads * a_hd
    m_width = m_heads * m_hd
    alpha = float((2 * depth) ** 0.25)
    assert a_width == pool_width == m_width == COL_TILE
    k_tile, v_tile = 1, 2
    n_tiles = w_in.shape[2] // COL_TILE

    w_o_b = w_o[0].astype(BF16)
    w_up_b = w_up[0].astype(BF16)
    w_down_b = w_down[0].astype(BF16)
    w_pool_b = w_pool[0].astype(BF16)

    mp = batch * seq
    xp = x_prompt.reshape(mp, d_model)
    u_col, mq_col, gate_col = 1, 2, 3
    p3 = _project(xp, w_in[0], lambda j: jnp.where(j == 0, 0, j + 2), n_tiles - 2,
                  tm=1024, name="in_proj_prompt")[0]
    k3, k_heads = _project(xp, w_in[0], lambda j: k_tile, 1, tm=1024, name="k_proj_prompt", heads=a_heads)
    v3, v_heads = _project(xp, w_in[0], lambda j: v_tile, 1, tm=1024, name="v_proj_prompt", heads=a_heads)
    attn_p = _moba_prompt(p3, k3[0], v3[0], batch=batch, seq=seq, heads=a_heads, hd=a_hd, q_col=0)
    pool_p = _pool_prompt(p3, w_pool_b, pool_scale, batch=batch, seq=seq, u_col=u_col)
    kv3 = _project(mem_prompt.reshape(batch * n_mem, d_model), w_mem_kv[0], lambda j: j, 2,
                   tm=batch * n_mem, name="mem_kv_prompt")[0]
    memo_p = _mem_prompt(p3, kv3, batch=batch, seq=seq, n_mem=n_mem, heads=m_heads, hd=m_hd, q_col=mq_col)
    mix_p = _merge(attn_p, pool_p, memo_p, w_br_attn[0], w_br_pool[0], w_br_mem[0], p3,
                   gate_col=gate_col, tm=512)
    h_p = _oproj(xp, mix_p, w_o_b, ln1_g, ln1_b, alpha=alpha, tm=512)
    y_p = _ffn(h_p, w_up_b, w_down_b, ln2_g, ln2_b, alpha=alpha, tm=512, tk=512)

    xs = x_sample.reshape(dec_batch, d_model)
    p3s = _project(xs, w_in[0], lambda j: j, n_tiles, tm=dec_batch, name="in_proj_sample")[0]
    q_s, k_s, v_s, u_s, mq_s = p3s[0], p3s[k_tile], p3s[v_tile], p3s[3], p3s[4]
    head_shape = (dec_batch, a_heads, a_hd)
    bsum = _block_sums(cache_k, page_table)
    picks = _select_blocks(bsum, q_s.reshape(head_shape), k_s.reshape(head_shape))
    idx = jnp.transpose(picks[:, :, :, 0], (0, 2, 1)).reshape(-1)
    row_tiled = (n_phys, page, a_heads, 1, a_hd)
    attn_s = _moba_sample(cache_k.reshape(row_tiled), cache_v.reshape(row_tiled), page_table, idx,
                          q_s, k_s, v_s, heads=a_heads, hd=a_hd)
    memo_s = _mem_sample(mq_s.reshape(dec_batch, m_heads, m_hd), cache_mem_k, cache_mem_v)
    pool_s = _pool_sample(u_s, jnp.transpose(state_pool[0], (1, 0, 2)), w_pool_b, pool_scale)
    mix_s = _merge(attn_s, pool_s, memo_s.reshape(dec_batch, m_width), w_br_attn[0], w_br_pool[0], w_br_mem[0],
                   p3s, gate_col=5, tm=dec_batch)
    h_s = _oproj(xs, mix_s, w_o_b, ln1_g, ln1_b, alpha=alpha, tm=dec_batch)
    y_s = _ffn(h_s, w_up_b, w_down_b, ln2_g, ln2_b, alpha=alpha, tm=dec_batch, tk=2048)

    kv_shape = (depth, batch, seq, a_heads, a_hd)
    mem_shape = (depth, batch, n_mem, m_heads, m_hd)
    new_pool_p = p3.reshape(-1, batch, seq, pool_width)[u_col, :, seq - pool_ctx:, :][None]
    new_pool_s = jnp.concatenate([state_pool[0][:, 1:, :], u_s[:, None, :]], axis=1)[None]
    skv_shape = (depth, dec_batch, dec_seq, a_heads, a_hd)
    return (y_p.reshape(batch, seq, d_model), y_s.reshape(dec_batch, dec_seq, d_model),
            k_heads.reshape(kv_shape), v_heads.reshape(kv_shape),
            kv3[0].reshape(mem_shape), kv3[1].reshape(mem_shape),
            new_pool_p,
            k_s.reshape(skv_shape), v_s.reshape(skv_shape),
            new_pool_s)
```

```python
import functools

import jax
import jax.numpy as jnp
from jax import lax
from jax.experimental import pallas as pl
from jax.experimental.pallas import tpu as pltpu
from jax.experimental.pallas import tpu_sc as plsc

MOBA_BLOCK = 256
MOBA_TOPK = 3
POOL_WINDOWS = (2, 4, 8, 16)
N_BRANCH = 3
LN_EPS = 1e-5
NEG = -1e30
PAGE_ROWS = 128

V7X_VMEM_LIMIT_BYTES = 60000 * 1024
V7X_LANES = 128
V7X_SC_LANES = 16
COL_TILE = 1024
FFN_SLAB = 512
SC_CHUNK_ROWS = 256

BF16 = jnp.bfloat16
F32 = jnp.float32
NT = (((1,), (1,)), ((), ()))


def _params(*semantics):
    return pltpu.CompilerParams(dimension_semantics=semantics,
                                vmem_limit_bytes=V7X_VMEM_LIMIT_BYTES)


def _layer_norm(x, g, b):
    mu = jnp.mean(x, axis=-1, keepdims=True)
    xc = x - mu
    var = jnp.mean(xc * xc, axis=-1, keepdims=True)
    return xc * lax.rsqrt(var + LN_EPS) * g + b


def _proj_kernel(x_ref, w_ref, o_ref, *rest, heads, emit_x):
    wb_ref = rest[-1]

    @pl.when(pl.program_id(1) == 0)
    def _():
        wb_ref[...] = w_ref[...].astype(BF16)

    xb = x_ref[...].astype(BF16)
    y = jnp.dot(xb, wb_ref[...], preferred_element_type=F32)
    o_ref[0] = y
    if heads:
        oh_ref = rest[0]
        tm, hd = y.shape[0], y.shape[1] // heads
        for h in range(heads):
            oh_ref[pl.ds(h, tm, stride=heads), :] = y[:, h * hd:(h + 1) * hd]
    if emit_x:
        rest[-2][...] = xb


def _project(x, w, col_of, n_cols, *, tm, name, heads=0, emit_x=False):
    m, k = x.shape
    tn = COL_TILE
    out_shape = [jax.ShapeDtypeStruct((n_cols, m, tn), F32)]
    out_specs = [pl.BlockSpec((1, tm, tn), lambda j, i: (j, i, 0))]
    if heads:
        assert n_cols == 1
        out_shape.append(jax.ShapeDtypeStruct((m * heads, tn // heads), F32))
        out_specs.append(pl.BlockSpec((tm * heads, tn // heads), lambda j, i: (i, 0)))
    if emit_x:
        assert n_cols == 1
        out_shape.append(jax.ShapeDtypeStruct((m, k), BF16))
        out_specs.append(pl.BlockSpec((tm, k), lambda j, i: (i, 0)))
    return pl.pallas_call(
        functools.partial(_proj_kernel, heads=heads, emit_x=emit_x),
        out_shape=out_shape,
        grid=(n_cols, m // tm),
        in_specs=[pl.BlockSpec((tm, k), lambda j, i: (i, 0)),
                  pl.BlockSpec((k, tn), lambda j, i: (0, col_of(j)))],
        out_specs=out_specs,
        scratch_shapes=[pltpu.VMEM((k, tn), BF16)],
        compiler_params=_params("arbitrary", "arbitrary"),
        name=name,
    )(x, w)


def _moba_prompt_kernel(q_ref, k_ref, v_ref, o_ref, *, seq, blk, topk):
    nb = seq // blk
    hd = q_ref.shape[-1]
    q = q_ref[0]
    k = k_ref[...]
    means = jnp.concatenate(
        [jnp.mean(k[n * blk:(n + 1) * blk], axis=0, keepdims=True) for n in range(nb)], axis=0)
    qs = q * (hd ** -0.5)
    kb = k.astype(BF16)
    vb = v_ref[...].astype(BF16)
    key_blk = lax.broadcasted_iota(jnp.int32, (seq, hd), 0) // blk
    lane = lax.broadcasted_iota(jnp.int32, (seq, hd), 1)
    k_ext = jnp.concatenate([kb, jnp.where(key_blk == lane, 1.0, 0.0).astype(BF16)], axis=1)
    row = lax.broadcasted_iota(jnp.int32, (blk, blk), 0)
    col = lax.broadcasted_iota(jnp.int32, (blk, blk), 1)
    causal = col <= row
    blk_id = lax.broadcasted_iota(jnp.int32, (nb, blk), 0)
    for i in range(nb):
        qi = qs[i * blk:(i + 1) * blk]
        if i > topk:
            gate = lax.dot_general(means, q[i * blk:(i + 1) * blk], NT,
                                   precision=lax.Precision.HIGHEST, preferred_element_type=F32)
            beaten = jnp.zeros((nb, blk), jnp.int32)
            for m in range(i):
                gm = gate[m:m + 1, :]
                beaten = beaten + jnp.where(m < blk_id, jnp.where(gm >= gate, 1, 0), jnp.where(gm > gate, 1, 0))
            bias = jnp.where((beaten < topk) | (blk_id >= i), 0.0, NEG)
            bias = jnp.concatenate([bias, jnp.zeros((hd - nb, blk), F32)], axis=0).T
            q_ext = jnp.concatenate([qi.astype(BF16), bias.astype(BF16)], axis=1)
            s = lax.dot_general(q_ext, k_ext[:(i + 1) * blk], NT, preferred_element_type=F32)
        else:
            s = lax.dot_general(qi.astype(BF16), kb[:(i + 1) * blk], NT, preferred_element_type=F32)
        own = jnp.where(causal, s[:, i * blk:(i + 1) * blk], NEG)
        s = jnp.concatenate([s[:, :i * blk], own], axis=1) if i else own
        mx = jnp.max(s, axis=-1, keepdims=True)
        p = jnp.exp(s - mx)
        den = jnp.sum(p, axis=-1, keepdims=True)
        o = jnp.dot(p.astype(BF16), vb[:(i + 1) * blk], preferred_element_type=F32)
        o_ref[i * blk:(i + 1) * blk, :] = (o / den).astype(o_ref.dtype)


def _moba_prompt(p3, k2, v2, *, batch, seq, heads, hd, q_col):
    assert heads * hd == COL_TILE
    kern = functools.partial(_moba_prompt_kernel, seq=seq, blk=MOBA_BLOCK, topk=MOBA_TOPK)
    kv = pl.BlockSpec((seq, hd), lambda b, h: (b, h))
    return pl.pallas_call(
        kern,
        out_shape=jax.ShapeDtypeStruct((batch * seq, heads * hd), BF16),
        grid=(batch, heads),
        in_specs=[pl.BlockSpec((1, seq, hd), lambda b, h: (q_col, b, h)), kv, kv],
        out_specs=kv,
        compiler_params=_params("arbitrary", "arbitrary"),
        name="moba_prompt",
    )(p3, k2, v2)


def _pool_prompt_kernel(u_ref, w_ref, sc_ref, o_ref, *, windows):
    seq = u_ref.shape[1]
    gc = w_ref.shape[-1]
    t = lax.broadcasted_iota(jnp.int32, (seq, 1), 0)
    for g, w in enumerate(windows):
        x = u_ref[0, :, g * gc:(g + 1) * gc]
        win = x
        width = 1
        while width < w:
            shifted = pltpu.roll(win, width, 0)
            win = win + jnp.where(t >= width, shifted, 0.0)
            width *= 2
        cnt = jnp.minimum(w, t + 1).astype(F32)
        mix = win / cnt - x
        y = jnp.dot(mix.astype(BF16), w_ref[g], preferred_element_type=F32)
        o_ref[:, g * gc:(g + 1) * gc] = (y * sc_ref[:, g * gc:(g + 1) * gc]).astype(o_ref.dtype)


def _pool_prompt(p3, w_pool, scale, *, batch, seq, u_col):
    groups, gc, _ = w_pool.shape
    width = groups * gc
    assert width == COL_TILE
    kern = functools.partial(_pool_prompt_kernel, windows=POOL_WINDOWS)
    return pl.pallas_call(
        kern,
        out_shape=jax.ShapeDtypeStruct((batch * seq, width), BF16),
        grid=(batch,),
        in_specs=[pl.BlockSpec((1, seq, width), lambda b: (u_col, b, 0)),
                  pl.BlockSpec((groups, gc, gc), lambda b: (0, 0, 0)),
                  pl.BlockSpec((1, width), lambda b: (0, 0))],
        out_specs=pl.BlockSpec((seq, width), lambda b: (b, 0)),
        compiler_params=_params("arbitrary"),
        name="pool_prompt",
    )(p3, w_pool, scale)


def _mem_prompt_kernel(q_ref, k_ref, v_ref, o_ref):
    hd = q_ref.shape[-1]
    qb = (q_ref[0] * (hd ** -0.5)).astype(BF16)
    s = lax.dot_general(qb, k_ref[0].astype(BF16), NT, preferred_element_type=F32)
    mx = jnp.max(s, axis=-1, keepdims=True)
    p = jnp.exp(s - mx)
    den = jnp.sum(p, axis=-1, keepdims=True)
    o = jnp.dot(p.astype(BF16), v_ref[0].astype(BF16), preferred_element_type=F32)
    o_ref[...] = (o / den).astype(o_ref.dtype)


def _mem_prompt(p3, kv3, *, batch, seq, n_mem, heads, hd, q_col):
    assert heads * hd == COL_TILE
    return pl.pallas_call(
        _mem_prompt_kernel,
        out_shape=jax.ShapeDtypeStruct((batch * seq, heads * hd), BF16),
        grid=(batch, heads),
        in_specs=[pl.BlockSpec((1, seq, hd), lambda b, h: (q_col, b, h)),
                  pl.BlockSpec((1, n_mem, hd), lambda b, h: (0, b, h)),
                  pl.BlockSpec((1, n_mem, hd), lambda b, h: (1, b, h))],
        out_specs=pl.BlockSpec((seq, hd), lambda b, h: (b, h)),
        compiler_params=_params("arbitrary", "arbitrary"),
        name="mem_prompt",
    )(p3, kv3, kv3)


def _merge_kernel(a_ref, p_ref, m_ref, wa_ref, wp_ref, wm_ref, ga_ref, gp_ref, gm_ref, o_ref,
                  wab_ref, wpb_ref, wmb_ref):
    @pl.when(pl.program_id(1) == 0)
    def _():
        wab_ref[...] = wa_ref[...].astype(BF16)
        wpb_ref[...] = wp_ref[...].astype(BF16)
        wmb_ref[...] = wm_ref[...].astype(BF16)

    def branch(x_ref, w_ref, g_ref):
        y = jnp.dot(x_ref[...].astype(BF16), w_ref[...], preferred_element_type=F32)
        return jax.nn.sigmoid(g_ref[0]) * y
    mix = branch(a_ref, wab_ref, ga_ref) + branch(p_ref, wpb_ref, gp_ref) + branch(m_ref, wmb_ref, gm_ref)
    o_ref[...] = mix.astype(o_ref.dtype)


def _merge(attn, pool, memo, w_ba, w_bp, w_bm, p3, *, gate_col, tm):
    m, kw = attn.shape
    d = w_ba.shape[1]
    tn = COL_TILE
    per_gate = d // tn
    lhs = pl.BlockSpec((tm, kw), lambda j, i: (i, 0))
    wsp = pl.BlockSpec((kw, tn), lambda j, i: (0, j))
    gsp = lambda c: pl.BlockSpec((1, tm, tn), lambda j, i: (gate_col + c * per_gate + j, i, 0))
    return pl.pallas_call(
        _merge_kernel,
        out_shape=jax.ShapeDtypeStruct((m, d), BF16),
        grid=(d // tn, m // tm),
        in_specs=[lhs, lhs, lhs, wsp, wsp, wsp, gsp(0), gsp(1), gsp(2)],
        out_specs=pl.BlockSpec((tm, tn), lambda j, i: (i, j)),
        scratch_shapes=[pltpu.VMEM((kw, tn), BF16)] * N_BRANCH,
        compiler_params=_params("arbitrary", "arbitrary"),
        name="merge",
    )(attn, pool, memo, w_ba, w_bp, w_bm, p3, p3, p3)


def _oproj_kernel(x_ref, mix_ref, w_ref, g_ref, b_ref, o_ref, *, alpha):
    y = jnp.dot(mix_ref[...], w_ref[...], preferred_element_type=F32)
    o_ref[...] = _layer_norm(alpha * x_ref[...] + y, g_ref[...], b_ref[...])


def _oproj(x, mix, w_o, g, b, *, alpha, tm):
    m, d = x.shape
    row = pl.BlockSpec((tm, d), lambda i: (i, 0))
    vec = pl.BlockSpec((1, d), lambda i: (0, 0))
    return pl.pallas_call(
        functools.partial(_oproj_kernel, alpha=alpha),
        out_shape=jax.ShapeDtypeStruct((m, d), F32),
        grid=(m // tm,),
        in_specs=[row, row, pl.BlockSpec((d, d), lambda i: (0, 0)), vec, vec],
        out_specs=row,
        compiler_params=_params("arbitrary"),
        name="oproj_ln1",
    )(x, mix, w_o, g, b)


def _ffn_kernel(h_ref, wu_ref, wd_ref, g_ref, b_ref, o_ref, hb_ref, *, alpha):
    kk = pl.program_id(1)

    @pl.when(kk == 0)
    def _():
        hb_ref[...] = h_ref[...].astype(BF16)
        o_ref[...] = jnp.zeros_like(o_ref)

    a = jnp.dot(hb_ref[...], wu_ref[...], preferred_element_type=F32)
    a = jnp.square(jnp.maximum(a, 0.0)).astype(BF16)
    d = o_ref.shape[1]
    for c in range(0, d, FFN_SLAB):
        o_ref[:, c:c + FFN_SLAB] += jnp.dot(a, wd_ref[:, c:c + FFN_SLAB], preferred_element_type=F32)

    @pl.when(kk == pl.num_programs(1) - 1)
    def _():
        o_ref[...] = _layer_norm(alpha * h_ref[...] + o_ref[...], g_ref[...], b_ref[...])


def _ffn(h, w_up, w_down, g, b, *, alpha, tm, tk):
    m, d = h.shape
    dff = w_up.shape[1]
    row = pl.BlockSpec((tm, d), lambda i, k: (i, 0))
    vec = pl.BlockSpec((1, d), lambda i, k: (0, 0))
    return pl.pallas_call(
        functools.partial(_ffn_kernel, alpha=alpha),
        out_shape=jax.ShapeDtypeStruct((m, d), F32),
        grid=(m // tm, dff // tk),
        in_specs=[row,
                  pl.BlockSpec((d, tk), lambda i, k: (0, k)),
                  pl.BlockSpec((tk, d), lambda i, k: (k, 0)),
                  vec, vec],
        out_specs=row,
        scratch_shapes=[pltpu.VMEM((tm, d), BF16)],
        compiler_params=_params("arbitrary", "arbitrary"),
        name="ffn_ln2",
    )(h, w_up, w_down, g, b)


def _block_sums(cache, page_table):
    _, n_phys, page, heads, hd = cache.shape
    bsz, n_pages = page_table.shape
    lanes = V7X_SC_LANES
    per_blk = MOBA_BLOCK // page
    n_blk = n_pages // per_blk
    rows_per_page = page * heads
    n_ch = rows_per_page // SC_CHUNK_ROWS
    sub = SC_CHUNK_ROWS // lanes
    assert sub % heads == 0 and (n_pages * n_ch * lanes) % V7X_LANES == 0
    items_per_blk = per_blk * n_ch
    n_items = n_pages * n_ch
    pieces = cache.reshape(n_phys * n_ch * lanes, sub, hd)
    chunk_id = page_table[:, :, None] * n_ch + jnp.arange(n_ch, dtype=jnp.int32)
    piece_id = chunk_id.reshape(bsz, n_items, 1) * lanes + jnp.arange(lanes, dtype=jnp.int32)
    piece_id = piece_id.reshape(bsz, n_items * lanes // V7X_LANES, V7X_LANES)
    ids_per_row = V7X_LANES // lanes
    mesh = plsc.VectorSubcoreMesh(core_axis_name="core", subcore_axis_name="subcore")
    assert bsz == mesh.num_cores * mesh.num_subcores, "one batch row per vector subcore"

    def kern(id_hbm, c_hbm, o_hbm, buf, acc, id_v, sems):
        b = lax.axis_index("core") * mesh.num_subcores + lax.axis_index("subcore")
        pltpu.sync_copy(id_hbm.at[b], id_v)

        def chunk_copy(item, slot):
            ids = id_v[item // ids_per_row, pl.ds((item % ids_per_row) * lanes, lanes)]
            return pltpu.make_async_copy(c_hbm.at[ids], buf.at[slot], sems.at[slot])

        chunk_copy(0, 0).start()

        @pl.loop(0, n_blk)
        def _(n):
            for h in range(heads):
                for c in range(hd // lanes):
                    acc[h, pl.ds(c * lanes, lanes)] = jnp.zeros((lanes,), F32)
            for j in range(items_per_blk):
                slot = j % 2
                item = n * items_per_blk + j
                chunk_copy(item, slot).wait()

                @pl.when(item + 1 < n_items)
                def _():
                    chunk_copy(item + 1, 1 - slot).start()

                @pl.loop(0, heads * (hd // lanes))
                def _(hc):
                    h = hc // (hd // lanes)
                    cols = pl.ds((hc % (hd // lanes)) * lanes, lanes)
                    rows = [(p, t * heads + h) for p in range(lanes) for t in range(sub // heads)]
                    parts = [buf[slot, p, r, cols] for p, r in rows[:4]]
                    for i, (p, r) in enumerate(rows[4:]):
                        parts[i % 4] = parts[i % 4] + buf[slot, p, r, cols]
                    plsc.addupdate(acc.at[h, cols], (parts[0] + parts[1]) + (parts[2] + parts[3]))
            pltpu.sync_copy(acc, o_hbm.at[pl.ds((b * n_blk + n) * heads, heads)])

    out = pl.kernel(
        kern,
        out_type=jax.ShapeDtypeStruct((bsz * n_blk * heads, hd), F32),
        mesh=mesh,
        scratch_types=[pltpu.VMEM((2, lanes, sub, hd), F32),
                       pltpu.VMEM((heads, hd), F32),
                       pltpu.VMEM((n_items * lanes // V7X_LANES, V7X_LANES), jnp.int32),
                       pltpu.SemaphoreType.DMA((2,))],
        name="block_sums",
    )(piece_id, pieces)
    return out.reshape(bsz, n_blk, heads, hd)


def _select_kernel(bs_ref, q_ref, kn_ref, idx_ref, *, blk, topk):
    n_past = bs_ref.shape[1]
    q = q_ref[0]
    past = jnp.sum((bs_ref[0] / blk) * q, axis=-1, keepdims=True)
    own = jnp.sum((kn_ref[0] / blk) * q, axis=-1, keepdims=True)[None]
    gate = jnp.concatenate([past, own], axis=0)
    n = lax.broadcasted_iota(jnp.int32, gate.shape, 0)
    gate = jnp.where(n < n_past, gate, NEG)
    picks = []
    for _ in range(topk):
        best = jnp.max(gate, axis=0, keepdims=True)
        first = jnp.min(jnp.where(gate == best, n, n_past + 1), axis=0, keepdims=True)
        picks.append(first)
        gate = jnp.where(n == first, -jnp.inf, gate)
    idx = jnp.concatenate(picks, axis=0)
    idx_ref[0] = jnp.broadcast_to(idx, idx_ref.shape[1:])


def _select_blocks(bsum, q, k_new):
    bsz, n_past, heads, hd = bsum.shape
    kern = functools.partial(_select_kernel, blk=MOBA_BLOCK, topk=MOBA_TOPK)
    tok = pl.BlockSpec((1, heads, hd), lambda b: (b, 0, 0))
    return pl.pallas_call(
        kern,
        out_shape=jax.ShapeDtypeStruct((bsz, MOBA_TOPK, heads, V7X_LANES), jnp.int32),
        grid=(bsz,),
        in_specs=[pl.BlockSpec((1, n_past, heads, hd), lambda b: (b, 0, 0, 0)), tok, tok],
        out_specs=pl.BlockSpec((1, MOBA_TOPK, heads, V7X_LANES), lambda b: (b, 0, 0, 0)),
        compiler_params=_params("arbitrary"),
        name="moba_select",
    )(bsum, q, k_new)


def _moba_sample_kernel(pt_ref, idx_ref, q_ref, kn_ref, vn_ref, *refs):
    del pt_ref, idx_ref
    o_ref = refs[-1]
    n_sel = (len(refs) - 1) // 2
    hd = q_ref.shape[-1]
    q = q_ref[0, 0] * (hd ** -0.5)
    ks = jnp.concatenate([r[0, :, 0, 0, :] for r in refs[:n_sel]], axis=0)
    vs = jnp.concatenate([r[0, :, 0, 0, :] for r in refs[n_sel:2 * n_sel]], axis=0)
    q8 = jnp.broadcast_to(q, (8, hd)).astype(BF16)
    s_sel = lax.dot_general(q8, ks.astype(BF16), NT, preferred_element_type=F32)
    s_own = jnp.sum(q * kn_ref[0, 0], axis=-1, keepdims=True)
    mx = jnp.maximum(jnp.max(s_sel, axis=-1, keepdims=True), s_own)
    p_sel = jnp.exp(s_sel - mx)
    p_own = jnp.exp(s_own - mx)
    den = jnp.sum(p_sel, axis=-1, keepdims=True) + p_own
    o = jnp.dot(p_sel.astype(BF16), vs.astype(BF16), preferred_element_type=F32)
    o = (o + p_own * vn_ref[0, 0]) / den
    o_ref[0, 0] = o[0:1]


def _moba_sample(cache_k, cache_v, page_table, idx, q, k_new, v_new, *, heads, hd):
    bsz = q.shape[0]
    per_blk = MOBA_BLOCK // PAGE_ROWS

    def cache_spec(j, r):
        def index(b, h, pt, ix):
            blk = ix[(b * heads + h) * MOBA_TOPK + j]
            return (pt[b, blk * per_blk + r], 0, h, 0, 0)
        return pl.BlockSpec((1, PAGE_ROWS, 1, 1, hd), index)

    tok = pl.BlockSpec((1, 1, 1, hd), lambda b, h, pt, ix: (b, h, 0, 0))
    sel_specs = [cache_spec(j, r) for j in range(MOBA_TOPK) for r in range(per_blk)]
    shape4 = (bsz, heads, 1, hd)
    out = pl.pallas_call(
        _moba_sample_kernel,
        out_shape=jax.ShapeDtypeStruct(shape4, F32),
        grid_spec=pltpu.PrefetchScalarGridSpec(
            num_scalar_prefetch=2,
            grid=(bsz, heads),
            in_specs=[tok, tok, tok] + sel_specs + sel_specs,
            out_specs=tok,
        ),
        compiler_params=_params("arbitrary", "arbitrary"),
        name="moba_sample",
    )(page_table, idx, q.reshape(shape4), k_new.reshape(shape4), v_new.reshape(shape4),
      *([cache_k] * len(sel_specs)), *([cache_v] * len(sel_specs)))
    return out.reshape(bsz, heads * hd)


def _mem_sample_kernel(q_ref, k_ref, v_ref, o_ref):
    hd = q_ref.shape[-1]
    q = q_ref[0] * (hd ** -0.5)
    s = jnp.sum(k_ref[0, 0] * q, axis=-1, keepdims=True)
    mx = jnp.max(s, axis=0, keepdims=True)
    p = jnp.exp(s - mx)
    den = jnp.sum(p, axis=0)
    o = jnp.sum(p * v_ref[0, 0], axis=0)
    o_ref[0] = o / den


def _mem_sample(q, mem_k, mem_v):
    _, bsz, n_mem, heads, hd = mem_k.shape
    tok = pl.BlockSpec((1, heads, hd), lambda b: (b, 0, 0))
    mem = pl.BlockSpec((1, 1, n_mem, heads, hd), lambda b: (0, b, 0, 0, 0))
    return pl.pallas_call(
        _mem_sample_kernel,
        out_shape=jax.ShapeDtypeStruct((bsz, heads, hd), F32),
        grid=(bsz,),
        in_specs=[tok, mem, mem],
        out_specs=tok,
        compiler_params=_params("arbitrary"),
        name="mem_sample",
    )(q, mem_k, mem_v)


def _pool_sample_kernel(u_ref, st_ref, w_ref, sc_ref, o_ref, *, windows):
    ctx = st_ref.shape[0]
    gc = w_ref.shape[-1]
    for g, w in enumerate(windows):
        sl = slice(g * gc, (g + 1) * gc)
        x = u_ref[:, sl]
        win = x
        for j in range(1, w):
            win = win + st_ref[ctx - j, :, sl]
        mix = win / float(w) - x
        y = jnp.dot(mix.astype(BF16), w_ref[g], preferred_element_type=F32)
        o_ref[:, sl] = (y * sc_ref[:, sl]).astype(o_ref.dtype)


def _pool_sample(u, state, w_pool, scale):
    bsz, width = u.shape
    return pl.pallas_call(
        functools.partial(_pool_sample_kernel, windows=POOL_WINDOWS),
        out_shape=jax.ShapeDtypeStruct((bsz, width), BF16),
        name="pool_sample",
        compiler_params=pltpu.CompilerParams(vmem_limit_bytes=V7X_VMEM_LIMIT_BYTES),
    )(u, state, w_pool, scale)


def kernel(x_prompt, x_sample, cache_k, cache_v, cache_mem_k, cache_mem_v, state_pool, page_table, mem_prompt, w_in, w_mem_kv, w_pool, pool_scale, w_br_attn, w_br_pool, w_br_mem, w_o, ln1_g, ln1_b, w_up, w_down, ln2_g, ln2_b):
    depth = w_in.shape[0]
    assert depth == 1, "single-layer trunk"
    batch, seq, d_model = x_prompt.shape
    dec_batch, dec_seq, _ = x_sample.shape
    assert dec_seq == 1
    _, n_phys, page, a_heads, a_hd = cache_k.shape
    assert page == PAGE_ROWS
    _, _, n_mem, m_heads, m_hd = cache_mem_k.shape
    pool_ctx, pool_width = state_pool.shape[2], state_pool.shape[3]
    a_width = a_heads * a_hd
    m_width = m_heads * m_hd
    alpha = float((2 * depth) ** 0.25)
    assert a_width == pool_width == m_width == COL_TILE
    k_tile, v_tile = 1, 2
    n_tiles = w_in.shape[2] // COL_TILE

    w_o_b = w_o[0].astype(BF16)
    w_up_b = w_up[0].astype(BF16)
    w_down_b = w_down[0].astype(BF16)
    w_pool_b = w_pool[0].astype(BF16)

    mp = batch * seq
    xp = x_prompt.reshape(mp, d_model)
    u_col, mq_col, gate_col = 1, 2, 3
    k3, k_heads, xb = _project(xp, w_in[0], lambda j: k_tile, 1, tm=1024, name="k_proj_prompt",
                               heads=a_heads, emit_x=True)
    v3, v_heads = _project(xb, w_in[0], lambda j: v_tile, 1, tm=1024, name="v_proj_prompt", heads=a_heads)
    p3 = _project(xb, w_in[0], lambda j: jnp.where(j == 0, 0, j + 2), n_tiles - 2,
                  tm=1024, name="in_proj_prompt")[0]
    attn_p = _moba_prompt(p3, k3[0], v3[0], batch=batch, seq=seq, heads=a_heads, hd=a_hd, q_col=0)
    pool_p = _pool_prompt(p3, w_pool_b, pool_scale, batch=batch, seq=seq, u_col=u_col)
    kv3 = _project(mem_prompt.reshape(batch * n_mem, d_model), w_mem_kv[0], lambda j: j, 2,
                   tm=batch * n_mem, name="mem_kv_prompt")[0]
    memo_p = _mem_prompt(p3, kv3, batch=batch, seq=seq, n_mem=n_mem, heads=m_heads, hd=m_hd, q_col=mq_col)
    mix_p = _merge(attn_p, pool_p, memo_p, w_br_attn[0], w_br_pool[0], w_br_mem[0], p3,
                   gate_col=gate_col, tm=512)
    h_p = _oproj(xp, mix_p, w_o_b, ln1_g, ln1_b, alpha=alpha, tm=512)
    y_p = _ffn(h_p, w_up_b, w_down_b, ln2_g, ln2_b, alpha=alpha, tm=1024, tk=512)

    xs = x_sample.reshape(dec_batch, d_model)
    p3s = _project(xs, w_in[0], lambda j: j, n_tiles, tm=dec_batch, name="in_proj_sample")[0]
    q_s, k_s, v_s, u_s, mq_s = p3s[0], p3s[k_tile], p3s[v_tile], p3s[3], p3s[4]
    head_shape = (dec_batch, a_heads, a_hd)
    bsum = _block_sums(cache_k, page_table)
    picks = _select_blocks(bsum, q_s.reshape(head_shape), k_s.reshape(head_shape))
    idx = jnp.transpose(picks[:, :, :, 0], (0, 2, 1)).reshape(-1)
    row_tiled = (n_phys, page, a_heads, 1, a_hd)
    attn_s = _moba_sample(cache_k.reshape(row_tiled), cache_v.reshape(row_tiled), page_table, idx,
                          q_s, k_s, v_s, heads=a_heads, hd=a_hd)
    memo_s = _mem_sample(mq_s.reshape(dec_batch, m_heads, m_hd), cache_mem_k, cache_mem_v)
    pool_s = _pool_sample(u_s, jnp.transpose(state_pool[0], (1, 0, 2)), w_pool_b, pool_scale)
    mix_s = _merge(attn_s, pool_s, memo_s.reshape(dec_batch, m_width), w_br_attn[0], w_br_pool[0], w_br_mem[0],
                   p3s, gate_col=5, tm=dec_batch)
    h_s = _oproj(xs, mix_s, w_o_b, ln1_g, ln1_b, alpha=alpha, tm=dec_batch)
    y_s = _ffn(h_s, w_up_b, w_down_b, ln2_g, ln2_b, alpha=alpha, tm=dec_batch, tk=2048)

    kv_shape = (depth, batch, seq, a_heads, a_hd)
    mem_shape = (depth, batch, n_mem, m_heads, m_hd)
    new_pool_p = p3.reshape(-1, batch, seq, pool_width)[u_col, :, seq - pool_ctx:, :][None]
    new_pool_s = jnp.concatenate([state_pool[0][:, 1:, :], u_s[:, None, :]], axis=1)[None]
    skv_shape = (depth, dec_batch, dec_seq, a_heads, a_hd)
    return (y_p.reshape(batch, seq, d_model), y_s.reshape(dec_batch, dec_seq, d_model),
            k_heads.reshape(kv_shape), v_heads.reshape(kv_shape),
            kv3[0].reshape(mem_shape), kv3[1].reshape(mem_shape),
            new_pool_p,
            k_s.reshape(skv_shape), v_s.reshape(skv_shape),
            new_pool_s)
```

```python
import functools

import jax
import jax.numpy as jnp
from jax import lax
from jax.experimental import pallas as pl
from jax.experimental.pallas import tpu as pltpu
from jax.experimental.pallas import tpu_sc as plsc

MOBA_BLOCK = 256
MOBA_TOPK = 3
POOL_WINDOWS = (2, 4, 8, 16)
N_BRANCH = 3
LN_EPS = 1e-5
NEG = -1e30
PAGE_ROWS = 128

V7X_VMEM_LIMIT_BYTES = 60000 * 1024
V7X_SC_LANES = 16
COL_TILE = 1024
FFN_SLAB = 512
SC_CHUNK_ROWS = 256

BF16 = jnp.bfloat16
F32 = jnp.float32
NT = (((1,), (1,)), ((), ()))


def _params(*semantics):
    return pltpu.CompilerParams(dimension_semantics=semantics,
                                vmem_limit_bytes=V7X_VMEM_LIMIT_BYTES)


def _layer_norm(x, g, b):
    mu = jnp.mean(x, axis=-1, keepdims=True)
    xc = x - mu
    var = jnp.mean(xc * xc, axis=-1, keepdims=True)
    return xc * lax.rsqrt(var + LN_EPS) * g + b


def _proj_kernel(x_ref, w_ref, o_ref, *rest, heads, emit_x):
    wb_ref = rest[-1]

    @pl.when(pl.program_id(1) == 0)
    def _():
        wb_ref[...] = w_ref[...].astype(BF16)

    xb = x_ref[...].astype(BF16)
    y = jnp.dot(xb, wb_ref[...], preferred_element_type=F32)
    o_ref[0] = y
    if heads:
        oh_ref = rest[0]
        tm, hd = y.shape[0], y.shape[1] // heads
        for h in range(heads):
            oh_ref[pl.ds(h, tm, stride=heads), :] = y[:, h * hd:(h + 1) * hd]
    if emit_x:
        rest[-2][...] = xb


def _project(x, w, col_of, n_cols, *, tm, name, heads=0, emit_x=False):
    m, k = x.shape
    tn = COL_TILE
    out_shape = [jax.ShapeDtypeStruct((n_cols, m, tn), F32)]
    out_specs = [pl.BlockSpec((1, tm, tn), lambda j, i: (j, i, 0))]
    if heads:
        assert n_cols == 1
        out_shape.append(jax.ShapeDtypeStruct((m * heads, tn // heads), F32))
        out_specs.append(pl.BlockSpec((tm * heads, tn // heads), lambda j, i: (i, 0)))
    if emit_x:
        assert n_cols == 1
        out_shape.append(jax.ShapeDtypeStruct((m, k), BF16))
        out_specs.append(pl.BlockSpec((tm, k), lambda j, i: (i, 0)))
    return pl.pallas_call(
        functools.partial(_proj_kernel, heads=heads, emit_x=emit_x),
        out_shape=out_shape,
        grid=(n_cols, m // tm),
        in_specs=[pl.BlockSpec((tm, k), lambda j, i: (i, 0)),
                  pl.BlockSpec((k, tn), lambda j, i: (0, col_of(j)))],
        out_specs=out_specs,
        scratch_shapes=[pltpu.VMEM((k, tn), BF16)],
        compiler_params=_params("arbitrary", "arbitrary"),
        name=name,
    )(x, w)


def _moba_prompt_kernel(q_ref, k_ref, v_ref, o_ref, *, seq, blk, topk):
    nb = seq // blk
    hd = q_ref.shape[-1]
    q = q_ref[0]
    k = k_ref[...]
    means = jnp.concatenate(
        [jnp.mean(k[n * blk:(n + 1) * blk], axis=0, keepdims=True) for n in range(nb)], axis=0)
    qs = q * (hd ** -0.5)
    kb = k.astype(BF16)
    vb = v_ref[...].astype(BF16)
    key_blk = lax.broadcasted_iota(jnp.int32, (seq, hd), 0) // blk
    lane = lax.broadcasted_iota(jnp.int32, (seq, hd), 1)
    k_ext = jnp.concatenate([kb, jnp.where(key_blk == lane, 1.0, 0.0).astype(BF16)], axis=1)
    row = lax.broadcasted_iota(jnp.int32, (blk, blk), 0)
    col = lax.broadcasted_iota(jnp.int32, (blk, blk), 1)
    causal = col <= row
    blk_id = lax.broadcasted_iota(jnp.int32, (nb, blk), 0)
    for i in range(nb):
        qi = qs[i * blk:(i + 1) * blk]
        if i > topk:
            gate = lax.dot_general(means, q[i * blk:(i + 1) * blk], NT,
                                   precision=lax.Precision.HIGHEST, preferred_element_type=F32)
            beaten = jnp.zeros((nb, blk), jnp.int32)
            for m in range(i):
                gm = gate[m:m + 1, :]
                beaten = beaten + jnp.where(m < blk_id, jnp.where(gm >= gate, 1, 0), jnp.where(gm > gate, 1, 0))
            bias = jnp.where((beaten < topk) | (blk_id >= i), 0.0, NEG)
            bias = jnp.concatenate([bias, jnp.zeros((hd - nb, blk), F32)], axis=0).T
            q_ext = jnp.concatenate([qi.astype(BF16), bias.astype(BF16)], axis=1)
            s = lax.dot_general(q_ext, k_ext[:(i + 1) * blk], NT, preferred_element_type=F32)
        else:
            s = lax.dot_general(qi.astype(BF16), kb[:(i + 1) * blk], NT, preferred_element_type=F32)
        own = jnp.where(causal, s[:, i * blk:(i + 1) * blk], NEG)
        s = jnp.concatenate([s[:, :i * blk], own], axis=1) if i else own
        mx = jnp.max(s, axis=-1, keepdims=True)
        p = jnp.exp(s - mx)
        den = jnp.sum(p, axis=-1, keepdims=True)
        o = jnp.dot(p.astype(BF16), vb[:(i + 1) * blk], preferred_element_type=F32)
        o_ref[i * blk:(i + 1) * blk, :] = (o / den).astype(o_ref.dtype)


def _moba_prompt(p3, k2, v2, *, batch, seq, heads, hd, q_col):
    assert heads * hd == COL_TILE
    kern = functools.partial(_moba_prompt_kernel, seq=seq, blk=MOBA_BLOCK, topk=MOBA_TOPK)
    kv = pl.BlockSpec((seq, hd), lambda b, h: (b, h))
    return pl.pallas_call(
        kern,
        out_shape=jax.ShapeDtypeStruct((batch * seq, heads * hd), BF16),
        grid=(batch, heads),
        in_specs=[pl.BlockSpec((1, seq, hd), lambda b, h: (q_col, b, h)), kv, kv],
        out_specs=kv,
        compiler_params=_params("arbitrary", "arbitrary"),
        name="moba_prompt",
    )(p3, k2, v2)


def _pool_prompt_kernel(u_ref, w_ref, sc_ref, o_ref, *, windows):
    seq = u_ref.shape[1]
    gc = w_ref.shape[-1]
    t = lax.broadcasted_iota(jnp.int32, (seq, 1), 0)
    for g, w in enumerate(windows):
        x = u_ref[0, :, g * gc:(g + 1) * gc]
        win = x
        width = 1
        while width < w:
            shifted = pltpu.roll(win, width, 0)
            win = win + jnp.where(t >= width, shifted, 0.0)
            width *= 2
        cnt = jnp.minimum(w, t + 1).astype(F32)
        mix = win / cnt - x
        y = jnp.dot(mix.astype(BF16), w_ref[g], preferred_element_type=F32)
        o_ref[:, g * gc:(g + 1) * gc] = (y * sc_ref[:, g * gc:(g + 1) * gc]).astype(o_ref.dtype)


def _pool_prompt(p3, w_pool, scale, *, batch, seq, u_col):
    groups, gc, _ = w_pool.shape
    width = groups * gc
    assert width == COL_TILE
    kern = functools.partial(_pool_prompt_kernel, windows=POOL_WINDOWS)
    return pl.pallas_call(
        kern,
        out_shape=jax.ShapeDtypeStruct((batch * seq, width), BF16),
        grid=(batch,),
        in_specs=[pl.BlockSpec((1, seq, width), lambda b: (u_col, b, 0)),
                  pl.BlockSpec((groups, gc, gc), lambda b: (0, 0, 0)),
                  pl.BlockSpec((1, width), lambda b: (0, 0))],
        out_specs=pl.BlockSpec((seq, width), lambda b: (b, 0)),
        compiler_params=_params("arbitrary"),
        name="pool_prompt",
    )(p3, w_pool, scale)


def _mem_prompt_kernel(q_ref, k_ref, v_ref, o_ref):
    hd = q_ref.shape[-1]
    qb = (q_ref[0] * (hd ** -0.5)).astype(BF16)
    s = lax.dot_general(qb, k_ref[0].astype(BF16), NT, preferred_element_type=F32)
    mx = jnp.max(s, axis=-1, keepdims=True)
    p = jnp.exp(s - mx)
    den = jnp.sum(p, axis=-1, keepdims=True)
    o = jnp.dot(p.astype(BF16), v_ref[0].astype(BF16), preferred_element_type=F32)
    o_ref[...] = (o / den).astype(o_ref.dtype)


def _mem_prompt(p3, kv3, *, batch, seq, n_mem, heads, hd, q_col):
    assert heads * hd == COL_TILE
    return pl.pallas_call(
        _mem_prompt_kernel,
        out_shape=jax.ShapeDtypeStruct((batch * seq, heads * hd), BF16),
        grid=(batch, heads),
        in_specs=[pl.BlockSpec((1, seq, hd), lambda b, h: (q_col, b, h)),
                  pl.BlockSpec((1, n_mem, hd), lambda b, h: (0, b, h)),
                  pl.BlockSpec((1, n_mem, hd), lambda b, h: (1, b, h))],
        out_specs=pl.BlockSpec((seq, hd), lambda b, h: (b, h)),
        compiler_params=_params("arbitrary", "arbitrary"),
        name="mem_prompt",
    )(p3, kv3, kv3)


def _merge_kernel(a_ref, p_ref, m_ref, wa_ref, wp_ref, wm_ref, ga_ref, gp_ref, gm_ref, o_ref,
                  wab_ref, wpb_ref, wmb_ref):
    @pl.when(pl.program_id(1) == 0)
    def _():
        wab_ref[...] = wa_ref[...].astype(BF16)
        wpb_ref[...] = wp_ref[...].astype(BF16)
        wmb_ref[...] = wm_ref[...].astype(BF16)

    def branch(x_ref, w_ref, g_ref):
        y = jnp.dot(x_ref[...].astype(BF16), w_ref[...], preferred_element_type=F32)
        return jax.nn.sigmoid(g_ref[0]) * y
    mix = branch(a_ref, wab_ref, ga_ref) + branch(p_ref, wpb_ref, gp_ref) + branch(m_ref, wmb_ref, gm_ref)
    o_ref[...] = mix.astype(o_ref.dtype)


def _merge(attn, pool, memo, w_ba, w_bp, w_bm, p3, *, gate_col, tm):
    m, kw = attn.shape
    d = w_ba.shape[1]
    tn = COL_TILE
    per_gate = d // tn
    lhs = pl.BlockSpec((tm, kw), lambda j, i: (i, 0))
    wsp = pl.BlockSpec((kw, tn), lambda j, i: (0, j))
    gsp = lambda c: pl.BlockSpec((1, tm, tn), lambda j, i: (gate_col + c * per_gate + j, i, 0))
    return pl.pallas_call(
        _merge_kernel,
        out_shape=jax.ShapeDtypeStruct((m, d), BF16),
        grid=(d // tn, m // tm),
        in_specs=[lhs, lhs, lhs, wsp, wsp, wsp, gsp(0), gsp(1), gsp(2)],
        out_specs=pl.BlockSpec((tm, tn), lambda j, i: (i, j)),
        scratch_shapes=[pltpu.VMEM((kw, tn), BF16)] * N_BRANCH,
        compiler_params=_params("arbitrary", "arbitrary"),
        name="merge",
    )(attn, pool, memo, w_ba, w_bp, w_bm, p3, p3, p3)


def _oproj_kernel(x_ref, mix_ref, w_ref, g_ref, b_ref, o_ref, *, alpha):
    y = jnp.dot(mix_ref[...], w_ref[...], preferred_element_type=F32)
    o_ref[...] = _layer_norm(alpha * x_ref[...] + y, g_ref[...], b_ref[...])


def _oproj(x, mix, w_o, g, b, *, alpha, tm):
    m, d = x.shape
    row = pl.BlockSpec((tm, d), lambda i: (i, 0))
    vec = pl.BlockSpec((1, d), lambda i: (0, 0))
    return pl.pallas_call(
        functools.partial(_oproj_kernel, alpha=alpha),
        out_shape=jax.ShapeDtypeStruct((m, d), F32),
        grid=(m // tm,),
        in_specs=[row, row, pl.BlockSpec((d, d), lambda i: (0, 0)), vec, vec],
        out_specs=row,
        compiler_params=_params("arbitrary"),
        name="oproj_ln1",
    )(x, mix, w_o, g, b)


def _ffn_kernel(h_ref, wu_ref, wd_ref, g_ref, b_ref, o_ref, hb_ref, *, alpha):
    kk = pl.program_id(1)

    @pl.when(kk == 0)
    def _():
        hb_ref[...] = h_ref[...].astype(BF16)
        o_ref[...] = jnp.zeros_like(o_ref)

    a = jnp.dot(hb_ref[...], wu_ref[...], preferred_element_type=F32)
    a = jnp.square(jnp.maximum(a, 0.0)).astype(BF16)
    d = o_ref.shape[1]
    for c in range(0, d, FFN_SLAB):
        o_ref[:, c:c + FFN_SLAB] += jnp.dot(a, wd_ref[:, c:c + FFN_SLAB], preferred_element_type=F32)

    @pl.when(kk == pl.num_programs(1) - 1)
    def _():
        o_ref[...] = _layer_norm(alpha * h_ref[...] + o_ref[...], g_ref[...], b_ref[...])


def _ffn(h, w_up, w_down, g, b, *, alpha, tm, tk):
    m, d = h.shape
    dff = w_up.shape[1]
    row = pl.BlockSpec((tm, d), lambda i, k: (i, 0))
    vec = pl.BlockSpec((1, d), lambda i, k: (0, 0))
    return pl.pallas_call(
        functools.partial(_ffn_kernel, alpha=alpha),
        out_shape=jax.ShapeDtypeStruct((m, d), F32),
        grid=(m // tm, dff // tk),
        in_specs=[row,
                  pl.BlockSpec((d, tk), lambda i, k: (0, k)),
                  pl.BlockSpec((tk, d), lambda i, k: (k, 0)),
                  vec, vec],
        out_specs=row,
        scratch_shapes=[pltpu.VMEM((tm, d), BF16)],
        compiler_params=_params("arbitrary", "arbitrary"),
        name="ffn_ln2",
    )(h, w_up, w_down, g, b)


def _moba_sample(cache_k, cache_v, page_table, q, k_new, v_new):
    _, n_phys, page, heads, hd = cache_k.shape
    bsz, n_pages = page_table.shape
    L = V7X_SC_LANES
    topk = MOBA_TOPK
    per_blk = MOBA_BLOCK // page
    n_blk = n_pages // per_blk
    rows_per_page = page * heads
    n_ch = rows_per_page // SC_CHUNK_ROWS
    items_per_blk = per_blk * n_ch
    n_items = n_pages * n_ch
    width = heads * hd
    hv = hd // L
    groups = page // L
    n_sel_pages = topk * per_blk
    gs = n_blk + 1
    scale = hd ** -0.5
    assert hd & (hd - 1) == 0 and n_pages % L == 0 and SC_CHUNK_ROWS % heads == 0
    krows = cache_k.reshape(n_phys * rows_per_page, hd)
    vrows = cache_v.reshape(n_phys * rows_per_page, hd)
    mesh = plsc.VectorSubcoreMesh(core_axis_name="core", subcore_axis_name="subcore")
    assert bsz == mesh.num_cores * mesh.num_subcores, "one batch row per vector subcore"

    def kern(kr_hbm, vr_hbm, pt_hbm, q_hbm, kn_hbm, vn_hbm, o_hbm,
             buf, acc, q_v, kn_v, vn_v, pt_v, rowbuf, idx_v, s_v, out_v, gate_s, sel_s, sems, rsem):
        b = lax.axis_index("core") * mesh.num_subcores + lax.axis_index("subcore")
        pltpu.sync_copy(q_hbm.at[b], q_v)
        pltpu.sync_copy(kn_hbm.at[b], kn_v)
        pltpu.sync_copy(vn_hbm.at[b], vn_v)
        pltpu.sync_copy(pt_hbm.at[b], pt_v)
        lane = lax.iota(jnp.int32, L)

        def phys_page(pg):
            group = pt_v[pl.ds((pg // L) * L, L)]
            return jnp.sum(jnp.where(lane == pg % L, group, 0))

        def chunk_start(item, slot):
            start = phys_page(item // n_ch) * rows_per_page + (item % n_ch) * SC_CHUNK_ROWS
            pltpu.make_async_copy(kr_hbm.at[pl.ds(pl.multiple_of(start, SC_CHUNK_ROWS), SC_CHUNK_ROWS)],
                                  buf.at[slot], sems.at[slot]).start()

        def chunk_wait(slot):
            pltpu.make_async_copy(kr_hbm.at[pl.ds(0, SC_CHUNK_ROWS)], buf.at[slot], sems.at[slot]).wait()

        chunk_start(0, 0)

        @pl.loop(0, n_blk)
        def _(n):
            for h in range(heads):
                for c in range(hv):
                    acc[h, pl.ds(c * L, L)] = jnp.zeros((L,), F32)
            for j in range(items_per_blk):
                slot = j % 2
                item = n * items_per_blk + j
                chunk_wait(slot)

                @pl.when(item + 1 < n_items)
                def _():
                    chunk_start(item + 1, 1 - slot)

                @pl.loop(0, heads * hv)
                def _(hc):
                    h = hc // hv
                    cols = pl.ds((hc % hv) * L, L)
                    rows = [t * heads + h for t in range(SC_CHUNK_ROWS // heads)]
                    parts = [buf[slot, r, cols] for r in rows[:4]]
                    for i, r in enumerate(rows[4:]):
                        parts[i % 4] = parts[i % 4] + buf[slot, r, cols]
                    plsc.addupdate(acc.at[h, cols], (parts[0] + parts[1]) + (parts[2] + parts[3]))
            for h in range(heads):
                prod = acc[h, pl.ds(0, L)] * q_v[pl.ds(h * hd, L)]
                for c in range(1, hv):
                    prod = prod + acc[h, pl.ds(c * L, L)] * q_v[pl.ds(h * hd + c * L, L)]
                gate_s[h * gs + n] = jnp.sum(prod / MOBA_BLOCK)

        own_blk = n_blk
        for h in range(heads):
            open_prod = kn_v[pl.ds(h * hd, L)] * q_v[pl.ds(h * hd, L)]
            for c in range(1, hv):
                open_prod = open_prod + kn_v[pl.ds(h * hd + c * L, L)] * q_v[pl.ds(h * hd + c * L, L)]
            gate_s[h * gs + n_blk] = jnp.where(n_blk < own_blk, jnp.sum(open_prod / MOBA_BLOCK), jnp.float32(NEG))
            for j in range(topk):
                def scan(n, carry, h=h):
                    best, bi = carry
                    g = gate_s[h * gs + n]
                    better = g > best
                    return jnp.where(better, g, best), jnp.where(better, n, bi)
                _, bi = lax.fori_loop(1, gs, scan, (gate_s[h * gs], jnp.int32(0)))
                gate_s[h * gs + bi] = jnp.float32(-jnp.inf)
                sel_s[h * topk + j] = bi

        def sel_page(h, jr):
            return sel_s[h * topk + jr // per_blk] * per_blk + jr % per_blk

        def rows_copy(src_hbm, slot):
            return pltpu.make_async_copy(src_hbm.at[idx_v.at[slot]], rowbuf.at[slot], rsem.at[slot])

        def rows_start(src_hbm, h, pg, slot):
            first = phys_page(pg) * page
            for c in range(groups):
                idx_v[slot, pl.ds(c * L, L)] = (first + c * L + lane) * heads + h
            rows_copy(src_hbm, slot).start()

        @pl.loop(0, heads)
        def _(h):
            rows_start(kr_hbm, h, sel_page(h, 0), 0)
            for jr in range(n_sel_pages):
                slot = jr % 2
                rows_copy(kr_hbm, slot).wait()
                if jr + 1 < n_sel_pages:
                    rows_start(kr_hbm, h, sel_page(h, jr + 1), 1 - slot)
                else:
                    rows_start(vr_hbm, h, sel_page(h, 0), 1 - slot)

                def dots(d, accs, slot=slot):
                    col = (d + lane) & (hd - 1)
                    qd = plsc.load_gather(q_v, [h * hd + col])
                    return tuple(accs[g] + plsc.load_gather(rowbuf.at[slot], [g * L + lane, col]) * qd
                                 for g in range(groups))
                accs = lax.fori_loop(0, hd, dots, tuple(jnp.zeros((L,), F32) for _ in range(groups)))
                for g in range(groups):
                    s_v[pl.ds(jr * page + g * L, L)] = accs[g] * scale
            own = q_v[pl.ds(h * hd, L)] * kn_v[pl.ds(h * hd, L)]
            for c in range(1, hv):
                own = own + q_v[pl.ds(h * hd + c * L, L)] * kn_v[pl.ds(h * hd + c * L, L)]
            s_own = jnp.sum(own) * scale
            n_vec = n_sel_pages * groups

            def vmax(i, m):
                return jnp.maximum(m, s_v[pl.ds(i * L, L)])
            mx = jnp.maximum(jnp.max(lax.fori_loop(0, n_vec, vmax, jnp.full((L,), NEG, F32))), s_own)

            def expsum(i, tot):
                p = jnp.exp(s_v[pl.ds(i * L, L)] - mx)
                s_v[pl.ds(i * L, L)] = p
                return tot + p
            p_own = jnp.max(jnp.exp(jnp.full((L,), s_own - mx, F32)))
            den = jnp.sum(lax.fori_loop(0, n_vec, expsum, jnp.zeros((L,), F32))) + p_own
            outs = tuple(p_own * vn_v[pl.ds(h * hd + c * L, L)] for c in range(hv))
            v_first = n_sel_pages % 2
            for jr in range(n_sel_pages):
                slot = (v_first + jr) % 2
                rows_copy(vr_hbm, slot).wait()
                if jr + 1 < n_sel_pages:
                    rows_start(vr_hbm, h, sel_page(h, jr + 1), 1 - slot)

                def pv(t, o, slot=slot, jr=jr):
                    p = plsc.load_gather(s_v, [jnp.full((L,), jr * page, jnp.int32) + t])
                    return tuple(o[c] + p * rowbuf[slot, t, pl.ds(c * L, L)] for c in range(hv))
                outs = lax.fori_loop(0, page, pv, outs)
            for c in range(hv):
                out_v[pl.ds(h * hd + c * L, L)] = outs[c] / den
        pltpu.sync_copy(out_v, o_hbm.at[b])

    return pl.kernel(
        kern,
        out_type=jax.ShapeDtypeStruct((bsz, width), F32),
        mesh=mesh,
        scratch_types=[pltpu.VMEM((2, SC_CHUNK_ROWS, hd), F32),
                       pltpu.VMEM((heads, hd), F32),
                       pltpu.VMEM((width,), F32),
                       pltpu.VMEM((width,), F32),
                       pltpu.VMEM((width,), F32),
                       pltpu.VMEM((n_pages,), jnp.int32),
                       pltpu.VMEM((2, page, hd), F32),
                       pltpu.VMEM((2, page), jnp.int32),
                       pltpu.VMEM((n_sel_pages * page,), F32),
                       pltpu.VMEM((width,), F32),
                       pltpu.SMEM((heads * gs,), F32),
                       pltpu.SMEM((heads * topk,), jnp.int32),
                       pltpu.SemaphoreType.DMA((2,)),
                       pltpu.SemaphoreType.DMA((2,))],
        compiler_params=pltpu.CompilerParams(needs_layout_passes=False),
        name="moba_sample",
    )(krows, vrows, page_table, q, k_new, v_new)


def _mem_sample_kernel(q_ref, k_ref, v_ref, o_ref):
    hd = q_ref.shape[-1]
    q = q_ref[0] * (hd ** -0.5)
    s = jnp.sum(k_ref[0, 0] * q, axis=-1, keepdims=True)
    mx = jnp.max(s, axis=0, keepdims=True)
    p = jnp.exp(s - mx)
    den = jnp.sum(p, axis=0)
    o = jnp.sum(p * v_ref[0, 0], axis=0)
    o_ref[0] = o / den


def _mem_sample(q, mem_k, mem_v):
    _, bsz, n_mem, heads, hd = mem_k.shape
    tok = pl.BlockSpec((1, heads, hd), lambda b: (b, 0, 0))
    mem = pl.BlockSpec((1, 1, n_mem, heads, hd), lambda b: (0, b, 0, 0, 0))
    return pl.pallas_call(
        _mem_sample_kernel,
        out_shape=jax.ShapeDtypeStruct((bsz, heads, hd), F32),
        grid=(bsz,),
        in_specs=[tok, mem, mem],
        out_specs=tok,
        compiler_params=_params("arbitrary"),
        name="mem_sample",
    )(q, mem_k, mem_v)


def _pool_sample_kernel(u_ref, st_ref, w_ref, sc_ref, o_ref, *, windows):
    ctx = st_ref.shape[0]
    gc = w_ref.shape[-1]
    for g, w in enumerate(windows):
        sl = slice(g * gc, (g + 1) * gc)
        x = u_ref[:, sl]
        win = x
        for j in range(1, w):
            win = win + st_ref[ctx - j, :, sl]
        mix = win / float(w) - x
        y = jnp.dot(mix.astype(BF16), w_ref[g], preferred_element_type=F32)
        o_ref[:, sl] = (y * sc_ref[:, sl]).astype(o_ref.dtype)


def _pool_sample(u, state, w_pool, scale):
    bsz, width = u.shape
    return pl.pallas_call(
        functools.partial(_pool_sample_kernel, windows=POOL_WINDOWS),
        out_shape=jax.ShapeDtypeStruct((bsz, width), BF16),
        name="pool_sample",
        compiler_params=pltpu.CompilerParams(vmem_limit_bytes=V7X_VMEM_LIMIT_BYTES),
    )(u, state, w_pool, scale)


def kernel(x_prompt, x_sample, cache_k, cache_v, cache_mem_k, cache_mem_v, state_pool, page_table, mem_prompt, w_in, w_mem_kv, w_pool, pool_scale, w_br_attn, w_br_pool, w_br_mem, w_o, ln1_g, ln1_b, w_up, w_down, ln2_g, ln2_b):
    depth = w_in.shape[0]
    assert depth == 1, "single-layer trunk"
    batch, seq, d_model = x_prompt.shape
    dec_batch, dec_seq, _ = x_sample.shape
    assert dec_seq == 1
    _, n_phys, page, a_heads, a_hd = cache_k.shape
    assert page == PAGE_ROWS
    _, _, n_mem, m_heads, m_hd = cache_mem_k.shape
    pool_ctx, pool_width = state_pool.shape[2], state_pool.shape[3]
    a_width = a_heads * a_hd
    m_width = m_heads * m_hd
    alpha = float((2 * depth) ** 0.25)
    assert a_width == pool_width == m_width == COL_TILE
    k_tile, v_tile = 1, 2
    n_tiles = w_in.shape[2] // COL_TILE

    w_o_b = w_o[0].astype(BF16)
    w_up_b = w_up[0].astype(BF16)
    w_down_b = w_down[0].astype(BF16)
    w_pool_b = w_pool[0].astype(BF16)

    xs = x_sample.reshape(dec_batch, d_model)
    p3s = _project(xs, w_in[0], lambda j: j, n_tiles, tm=dec_batch, name="in_proj_sample")[0]
    q_s, k_s, v_s, u_s, mq_s = p3s[0], p3s[k_tile], p3s[v_tile], p3s[3], p3s[4]

    mp = batch * seq
    xp = x_prompt.reshape(mp, d_model)
    u_col, mq_col, gate_col = 1, 2, 3
    k3, k_heads, xb = _project(xp, w_in[0], lambda j: k_tile, 1, tm=1024, name="k_proj_prompt",
                               heads=a_heads, emit_x=True)
    v3, v_heads = _project(xb, w_in[0], lambda j: v_tile, 1, tm=1024, name="v_proj_prompt", heads=a_heads)
    p3 = _project(xb, w_in[0], lambda j: jnp.where(j == 0, 0, j + 2), n_tiles - 2,
                  tm=1024, name="in_proj_prompt")[0]
    attn_p = _moba_prompt(p3, k3[0], v3[0], batch=batch, seq=seq, heads=a_heads, hd=a_hd, q_col=0)
    pool_p = _pool_prompt(p3, w_pool_b, pool_scale, batch=batch, seq=seq, u_col=u_col)
    kv3 = _project(mem_prompt.reshape(batch * n_mem, d_model), w_mem_kv[0], lambda j: j, 2,
                   tm=batch * n_mem, name="mem_kv_prompt")[0]
    memo_p = _mem_prompt(p3, kv3, batch=batch, seq=seq, n_mem=n_mem, heads=m_heads, hd=m_hd, q_col=mq_col)
    mix_p = _merge(attn_p, pool_p, memo_p, w_br_attn[0], w_br_pool[0], w_br_mem[0], p3,
                   gate_col=gate_col, tm=512)
    h_p = _oproj(xp, mix_p, w_o_b, ln1_g, ln1_b, alpha=alpha, tm=512)
    y_p = _ffn(h_p, w_up_b, w_down_b, ln2_g, ln2_b, alpha=alpha, tm=1024, tk=512)

    memo_s = _mem_sample(mq_s.reshape(dec_batch, m_heads, m_hd), cache_mem_k, cache_mem_v)
    pool_s = _pool_sample(u_s, jnp.transpose(state_pool[0], (1, 0, 2)), w_pool_b, pool_scale)
    attn_s = _moba_sample(cache_k, cache_v, page_table, q_s, k_s, v_s)
    mix_s = _merge(attn_s, pool_s, memo_s.reshape(dec_batch, m_width), w_br_attn[0], w_br_pool[0], w_br_mem[0],
                   p3s, gate_col=5, tm=dec_batch)
    h_s = _oproj(xs, mix_s, w_o_b, ln1_g, ln1_b, alpha=alpha, tm=dec_batch)
    y_s = _ffn(h_s, w_up_b, w_down_b, ln2_g, ln2_b, alpha=alpha, tm=dec_batch, tk=2048)

    kv_shape = (depth, batch, seq, a_heads, a_hd)
    mem_shape = (depth, batch, n_mem, m_heads, m_hd)
    new_pool_p = p3.reshape(-1, batch, seq, pool_width)[u_col, :, seq - pool_ctx:, :][None]
    new_pool_s = jnp.concatenate([state_pool[0][:, 1:, :], u_s[:, None, :]], axis=1)[None]
    skv_shape = (depth, dec_batch, dec_seq, a_heads, a_hd)
    return (y_p.reshape(batch, seq, d_model), y_s.reshape(dec_batch, dec_seq, d_model),
            k_heads.reshape(kv_shape), v_heads.reshape(kv_shape),
            kv3[0].reshape(mem_shape), kv3[1].reshape(mem_shape),
            new_pool_p,
            k_s.reshape(skv_shape), v_s.reshape(skv_shape),
            new_pool_s)
```

```python
import functools

import jax
import jax.numpy as jnp
from jax import lax
from jax.experimental import pallas as pl
from jax.experimental.pallas import tpu as pltpu
from jax.experimental.pallas import tpu_sc as plsc

MOBA_BLOCK = 256
MOBA_TOPK = 3
POOL_WINDOWS = (2, 4, 8, 16)
N_BRANCH = 3
LN_EPS = 1e-5
NEG = -1e30
PAGE_ROWS = 128

V7X_VMEM_LIMIT_BYTES = 60000 * 1024
V7X_SC_LANES = 16
COL_TILE = 1024
FFN_SLAB = 512
CAST_PARTS = 4
BF16_ROWS = 16
SC_CHUNK_ROWS = 256

BF16 = jnp.bfloat16
F32 = jnp.float32
NT = (((1,), (1,)), ((), ()))


def _params(*semantics):
    return pltpu.CompilerParams(dimension_semantics=semantics,
                                vmem_limit_bytes=V7X_VMEM_LIMIT_BYTES)


def _layer_norm(x, g, b):
    mu = jnp.mean(x, axis=-1, keepdims=True)
    xc = x - mu
    var = jnp.mean(xc * xc, axis=-1, keepdims=True)
    return xc * lax.rsqrt(var + LN_EPS) * g + b


def _proj_kernel(x_ref, w_ref, o_ref, *rest, heads, emit_x):
    wb_ref = rest[-1]

    @pl.when(pl.program_id(1) == 0)
    def _():
        wb_ref[...] = w_ref[...].astype(BF16)

    xb = x_ref[...].astype(BF16)
    y = jnp.dot(xb, wb_ref[...], preferred_element_type=F32)
    o_ref[0] = y
    if heads:
        oh_ref = rest[0]
        tm, hd = y.shape[0], y.shape[1] // heads
        for h in range(heads):
            oh_ref[pl.ds(h, tm, stride=heads), :] = y[:, h * hd:(h + 1) * hd]
    if emit_x:
        rest[-2][...] = xb


def _project(x, w, col_of, n_cols, *, tm, name, heads=0, emit_x=False):
    m, k = x.shape
    tn = COL_TILE
    out_shape = [jax.ShapeDtypeStruct((n_cols, m, tn), F32)]
    out_specs = [pl.BlockSpec((1, tm, tn), lambda j, i: (j, i, 0))]
    if heads:
        assert n_cols == 1
        out_shape.append(jax.ShapeDtypeStruct((m * heads, tn // heads), F32))
        out_specs.append(pl.BlockSpec((tm * heads, tn // heads), lambda j, i: (i, 0)))
    if emit_x:
        assert n_cols == 1
        out_shape.append(jax.ShapeDtypeStruct((m, k), BF16))
        out_specs.append(pl.BlockSpec((tm, k), lambda j, i: (i, 0)))
    return pl.pallas_call(
        functools.partial(_proj_kernel, heads=heads, emit_x=emit_x),
        out_shape=out_shape,
        grid=(n_cols, m // tm),
        in_specs=[pl.BlockSpec((tm, k), lambda j, i: (i, 0)),
                  pl.BlockSpec((k, tn), lambda j, i: (0, col_of(j)))],
        out_specs=out_specs,
        scratch_shapes=[pltpu.VMEM((k, tn), BF16)],
        compiler_params=_params("arbitrary", "arbitrary"),
        name=name,
    )(x, w)


def _moba_prompt_kernel(q_ref, k_ref, v_ref, *rest, seq, blk, topk):
    n_cast = (len(rest) - 1) // 2
    cast_src, o_ref, cast_dst = rest[:n_cast], rest[n_cast], rest[n_cast + 1:]
    nb = seq // blk
    hd = q_ref.shape[-1]
    q = q_ref[0]
    k = k_ref[...]
    means = jnp.concatenate(
        [jnp.mean(k[n * blk:(n + 1) * blk], axis=0, keepdims=True) for n in range(nb)], axis=0)
    qs = q * (hd ** -0.5)
    kb = k.astype(BF16)
    vb = v_ref[...].astype(BF16)
    key_blk = lax.broadcasted_iota(jnp.int32, (seq, hd), 0) // blk
    lane = lax.broadcasted_iota(jnp.int32, (seq, hd), 1)
    k_ext = jnp.concatenate([kb, jnp.where(key_blk == lane, 1.0, 0.0).astype(BF16)], axis=1)
    row = lax.broadcasted_iota(jnp.int32, (blk, blk), 0)
    col = lax.broadcasted_iota(jnp.int32, (blk, blk), 1)
    causal = col <= row
    blk_id = lax.broadcasted_iota(jnp.int32, (nb, blk), 0)
    for i in range(nb):
        qi = qs[i * blk:(i + 1) * blk]
        if i > topk:
            gate = lax.dot_general(means, q[i * blk:(i + 1) * blk], NT,
                                   precision=lax.Precision.HIGHEST, preferred_element_type=F32)
            beaten = jnp.zeros((nb, blk), jnp.int32)
            for m in range(i):
                gm = gate[m:m + 1, :]
                beaten = beaten + jnp.where(m < blk_id, jnp.where(gm >= gate, 1, 0), jnp.where(gm > gate, 1, 0))
            bias = jnp.where((beaten < topk) | (blk_id >= i), 0.0, NEG)
            bias = jnp.concatenate([bias, jnp.zeros((hd - nb, blk), F32)], axis=0).T
            q_ext = jnp.concatenate([qi.astype(BF16), bias.astype(BF16)], axis=1)
            s = lax.dot_general(q_ext, k_ext[:(i + 1) * blk], NT, preferred_element_type=F32)
        else:
            s = lax.dot_general(qi.astype(BF16), kb[:(i + 1) * blk], NT, preferred_element_type=F32)
        own = jnp.where(causal, s[:, i * blk:(i + 1) * blk], NEG)
        s = jnp.concatenate([s[:, :i * blk], own], axis=1) if i else own
        mx = jnp.max(s, axis=-1, keepdims=True)
        p = jnp.exp(s - mx)
        den = jnp.sum(p, axis=-1, keepdims=True)
        o = jnp.dot(p.astype(BF16), vb[:(i + 1) * blk], preferred_element_type=F32)
        o_ref[i * blk:(i + 1) * blk, :] = (o / den).astype(o_ref.dtype)
        if n_cast and i % (nb // CAST_PARTS) == nb // CAST_PARTS - 1:
            part = i // (nb // CAST_PARTS)
            for src, dst in zip(cast_src, cast_dst):
                rows = src.shape[0] // CAST_PARTS
                dst[part * rows:(part + 1) * rows, :] = src[part * rows:(part + 1) * rows, :].astype(BF16)


def _moba_prompt(p3, k2, v2, *, batch, seq, heads, hd, q_col, cast=()):
    assert heads * hd == COL_TILE
    steps = batch * heads
    kern = functools.partial(_moba_prompt_kernel, seq=seq, blk=MOBA_BLOCK, topk=MOBA_TOPK)
    kv = pl.BlockSpec((seq, hd), lambda b, h: (b, h))
    slab = lambda w: pl.BlockSpec((w.shape[0] // steps, w.shape[1]), lambda b, h: (b * heads + h, 0))
    assert all(w.shape[0] % (steps * CAST_PARTS * BF16_ROWS) == 0 for w in cast)
    out = pl.pallas_call(
        kern,
        out_shape=[jax.ShapeDtypeStruct((batch * seq, heads * hd), BF16)]
        + [jax.ShapeDtypeStruct(w.shape, BF16) for w in cast],
        grid=(batch, heads),
        in_specs=[pl.BlockSpec((1, seq, hd), lambda b, h: (q_col, b, h)), kv, kv] + [slab(w) for w in cast],
        out_specs=[kv] + [slab(w) for w in cast],
        compiler_params=_params("arbitrary", "arbitrary"),
        name="moba_prompt",
    )(p3, k2, v2, *cast)
    return out[0], out[1:]


def _pool_prompt_kernel(u_ref, w_ref, sc_ref, o_ref, *, windows):
    seq = u_ref.shape[1]
    gc = w_ref.shape[-1]
    t = lax.broadcasted_iota(jnp.int32, (seq, 1), 0)
    for g, w in enumerate(windows):
        x = u_ref[0, :, g * gc:(g + 1) * gc]
        win = x
        width = 1
        while width < w:
            shifted = pltpu.roll(win, width, 0)
            win = win + jnp.where(t >= width, shifted, 0.0)
            width *= 2
        cnt = jnp.minimum(w, t + 1).astype(F32)
        mix = win / cnt - x
        y = jnp.dot(mix.astype(BF16), w_ref[g], preferred_element_type=F32)
        o_ref[:, g * gc:(g + 1) * gc] = (y * sc_ref[:, g * gc:(g + 1) * gc]).astype(o_ref.dtype)


def _pool_prompt(p3, w_pool, scale, *, batch, seq, u_col):
    groups, gc, _ = w_pool.shape
    width = groups * gc
    assert width == COL_TILE
    kern = functools.partial(_pool_prompt_kernel, windows=POOL_WINDOWS)
    return pl.pallas_call(
        kern,
        out_shape=jax.ShapeDtypeStruct((batch * seq, width), BF16),
        grid=(batch,),
        in_specs=[pl.BlockSpec((1, seq, width), lambda b: (u_col, b, 0)),
                  pl.BlockSpec((groups, gc, gc), lambda b: (0, 0, 0)),
                  pl.BlockSpec((1, width), lambda b: (0, 0))],
        out_specs=pl.BlockSpec((seq, width), lambda b: (b, 0)),
        compiler_params=_params("arbitrary"),
        name="pool_prompt",
    )(p3, w_pool, scale)


def _mem_prompt_kernel(q_ref, k_ref, v_ref, o_ref):
    hd = q_ref.shape[-1]
    qb = (q_ref[0] * (hd ** -0.5)).astype(BF16)
    s = lax.dot_general(qb, k_ref[0].astype(BF16), NT, preferred_element_type=F32)
    mx = jnp.max(s, axis=-1, keepdims=True)
    p = jnp.exp(s - mx)
    den = jnp.sum(p, axis=-1, keepdims=True)
    o = jnp.dot(p.astype(BF16), v_ref[0].astype(BF16), preferred_element_type=F32)
    o_ref[...] = (o / den).astype(o_ref.dtype)


def _mem_prompt(p3, kv3, *, batch, seq, n_mem, heads, hd, q_col):
    assert heads * hd == COL_TILE
    return pl.pallas_call(
        _mem_prompt_kernel,
        out_shape=jax.ShapeDtypeStruct((batch * seq, heads * hd), BF16),
        grid=(batch, heads),
        in_specs=[pl.BlockSpec((1, seq, hd), lambda b, h: (q_col, b, h)),
                  pl.BlockSpec((1, n_mem, hd), lambda b, h: (0, b, h)),
                  pl.BlockSpec((1, n_mem, hd), lambda b, h: (1, b, h))],
        out_specs=pl.BlockSpec((seq, hd), lambda b, h: (b, h)),
        compiler_params=_params("arbitrary", "arbitrary"),
        name="mem_prompt",
    )(p3, kv3, kv3)


def _merge_kernel(a_ref, p_ref, m_ref, wa_ref, wp_ref, wm_ref, ga_ref, gp_ref, gm_ref, o_ref,
                  wab_ref, wpb_ref, wmb_ref):
    @pl.when(pl.program_id(1) == 0)
    def _():
        wab_ref[...] = wa_ref[...].astype(BF16)
        wpb_ref[...] = wp_ref[...].astype(BF16)
        wmb_ref[...] = wm_ref[...].astype(BF16)

    def branch(x_ref, w_ref, g_ref):
        y = jnp.dot(x_ref[...].astype(BF16), w_ref[...], preferred_element_type=F32)
        return jax.nn.sigmoid(g_ref[0]) * y
    mix = branch(a_ref, wab_ref, ga_ref) + branch(p_ref, wpb_ref, gp_ref) + branch(m_ref, wmb_ref, gm_ref)
    o_ref[...] = mix.astype(o_ref.dtype)


def _merge(attn, pool, memo, w_ba, w_bp, w_bm, p3, *, gate_col, tm):
    m, kw = attn.shape
    d = w_ba.shape[1]
    tn = COL_TILE
    per_gate = d // tn
    lhs = pl.BlockSpec((tm, kw), lambda j, i: (i, 0))
    wsp = pl.BlockSpec((kw, tn), lambda j, i: (0, j))
    gsp = lambda c: pl.BlockSpec((1, tm, tn), lambda j, i: (gate_col + c * per_gate + j, i, 0))
    return pl.pallas_call(
        _merge_kernel,
        out_shape=jax.ShapeDtypeStruct((m, d), BF16),
        grid=(d // tn, m // tm),
        in_specs=[lhs, lhs, lhs, wsp, wsp, wsp, gsp(0), gsp(1), gsp(2)],
        out_specs=pl.BlockSpec((tm, tn), lambda j, i: (i, j)),
        scratch_shapes=[pltpu.VMEM((kw, tn), BF16)] * N_BRANCH,
        compiler_params=_params("arbitrary", "arbitrary"),
        name="merge",
    )(attn, pool, memo, w_ba, w_bp, w_bm, p3, p3, p3)


def _oproj_kernel(x_ref, mix_ref, w_ref, g_ref, b_ref, o_ref, *, alpha):
    y = jnp.dot(mix_ref[...], w_ref[...], preferred_element_type=F32)
    o_ref[...] = _layer_norm(alpha * x_ref[...] + y, g_ref[...], b_ref[...])


def _oproj(x, mix, w_o, g, b, *, alpha, tm):
    m, d = x.shape
    row = pl.BlockSpec((tm, d), lambda i: (i, 0))
    vec = pl.BlockSpec((1, d), lambda i: (0, 0))
    return pl.pallas_call(
        functools.partial(_oproj_kernel, alpha=alpha),
        out_shape=jax.ShapeDtypeStruct((m, d), F32),
        grid=(m // tm,),
        in_specs=[row, row, pl.BlockSpec((d, d), lambda i: (0, 0)), vec, vec],
        out_specs=row,
        compiler_params=_params("arbitrary"),
        name="oproj_ln1",
    )(x, mix, w_o, g, b)


def _ffn_kernel(h_ref, wu_ref, wd_ref, g_ref, b_ref, o_ref, hb_ref, *, alpha):
    kk = pl.program_id(1)

    @pl.when(kk == 0)
    def _():
        hb_ref[...] = h_ref[...].astype(BF16)
        o_ref[...] = jnp.zeros_like(o_ref)

    a = jnp.dot(hb_ref[...], wu_ref[...], preferred_element_type=F32)
    a = jnp.square(jnp.maximum(a, 0.0)).astype(BF16)
    d = o_ref.shape[1]
    for c in range(0, d, FFN_SLAB):
        o_ref[:, c:c + FFN_SLAB] += jnp.dot(a, wd_ref[:, c:c + FFN_SLAB], preferred_element_type=F32)

    @pl.when(kk == pl.num_programs(1) - 1)
    def _():
        o_ref[...] = _layer_norm(alpha * h_ref[...] + o_ref[...], g_ref[...], b_ref[...])


def _ffn(h, w_up, w_down, g, b, *, alpha, tm, tk):
    m, d = h.shape
    dff = w_up.shape[1]
    row = pl.BlockSpec((tm, d), lambda i, k: (i, 0))
    vec = pl.BlockSpec((1, d), lambda i, k: (0, 0))
    return pl.pallas_call(
        functools.partial(_ffn_kernel, alpha=alpha),
        out_shape=jax.ShapeDtypeStruct((m, d), F32),
        grid=(m // tm, dff // tk),
        in_specs=[row,
                  pl.BlockSpec((d, tk), lambda i, k: (0, k)),
                  pl.BlockSpec((tk, d), lambda i, k: (k, 0)),
                  vec, vec],
        out_specs=row,
        scratch_shapes=[pltpu.VMEM((tm, d), BF16)],
        compiler_params=_params("arbitrary", "arbitrary"),
        name="ffn_ln2",
    )(h, w_up, w_down, g, b)


def _moba_sample(cache_k, cache_v, page_table, q, k_new, v_new):
    _, n_phys, page, heads, hd = cache_k.shape
    bsz, n_pages = page_table.shape
    L = V7X_SC_LANES
    topk = MOBA_TOPK
    per_blk = MOBA_BLOCK // page
    n_blk = n_pages // per_blk
    rows_per_page = page * heads
    n_ch = rows_per_page // SC_CHUNK_ROWS
    items_per_blk = per_blk * n_ch
    n_items = n_pages * n_ch
    width = heads * hd
    hv = hd // L
    groups = page // L
    n_sel_pages = topk * per_blk
    gs = n_blk + 1
    scale = hd ** -0.5
    assert hd & (hd - 1) == 0 and n_pages % L == 0 and SC_CHUNK_ROWS % heads == 0
    krows = cache_k.reshape(n_phys * rows_per_page, hd)
    vrows = cache_v.reshape(n_phys * rows_per_page, hd)
    mesh = plsc.VectorSubcoreMesh(core_axis_name="core", subcore_axis_name="subcore")
    assert bsz == mesh.num_cores * mesh.num_subcores, "one batch row per vector subcore"

    def kern(kr_hbm, vr_hbm, pt_hbm, q_hbm, kn_hbm, vn_hbm, o_hbm,
             buf, acc, q_v, kn_v, vn_v, pt_v, rowbuf, idx_v, s_v, out_v, gate_s, sel_s, sems, rsem):
        b = lax.axis_index("core") * mesh.num_subcores + lax.axis_index("subcore")
        pltpu.sync_copy(q_hbm.at[b], q_v)
        pltpu.sync_copy(kn_hbm.at[b], kn_v)
        pltpu.sync_copy(vn_hbm.at[b], vn_v)
        pltpu.sync_copy(pt_hbm.at[b], pt_v)
        lane = lax.iota(jnp.int32, L)

        def phys_page(pg):
            group = pt_v[pl.ds((pg // L) * L, L)]
            return jnp.sum(jnp.where(lane == pg % L, group, 0))

        def chunk_start(item, slot):
            start = phys_page(item // n_ch) * rows_per_page + (item % n_ch) * SC_CHUNK_ROWS
            pltpu.make_async_copy(kr_hbm.at[pl.ds(pl.multiple_of(start, SC_CHUNK_ROWS), SC_CHUNK_ROWS)],
                                  buf.at[slot], sems.at[slot]).start()

        def chunk_wait(slot):
            pltpu.make_async_copy(kr_hbm.at[pl.ds(0, SC_CHUNK_ROWS)], buf.at[slot], sems.at[slot]).wait()

        chunk_start(0, 0)

        @pl.loop(0, n_blk)
        def _(n):
            for h in range(heads):
                for c in range(hv):
                    acc[h, pl.ds(c * L, L)] = jnp.zeros((L,), F32)
            for j in range(items_per_blk):
                slot = j % 2
                item = n * items_per_blk + j
                chunk_wait(slot)

                @pl.when(item + 1 < n_items)
                def _():
                    chunk_start(item + 1, 1 - slot)

                @pl.loop(0, heads * hv)
                def _(hc):
                    h = hc // hv
                    cols = pl.ds((hc % hv) * L, L)
                    rows = [t * heads + h for t in range(SC_CHUNK_ROWS // heads)]
                    parts = [buf[slot, r, cols] for r in rows[:4]]
                    for i, r in enumerate(rows[4:]):
                        parts[i % 4] = parts[i % 4] + buf[slot, r, cols]
                    plsc.addupdate(acc.at[h, cols], (parts[0] + parts[1]) + (parts[2] + parts[3]))
            for h in range(heads):
                prod = acc[h, pl.ds(0, L)] * q_v[pl.ds(h * hd, L)]
                for c in range(1, hv):
                    prod = prod + acc[h, pl.ds(c * L, L)] * q_v[pl.ds(h * hd + c * L, L)]
                gate_s[h * gs + n] = jnp.sum(prod / MOBA_BLOCK)

        own_blk = n_blk
        for h in range(heads):
            open_prod = kn_v[pl.ds(h * hd, L)] * q_v[pl.ds(h * hd, L)]
            for c in range(1, hv):
                open_prod = open_prod + kn_v[pl.ds(h * hd + c * L, L)] * q_v[pl.ds(h * hd + c * L, L)]
            gate_s[h * gs + n_blk] = jnp.where(n_blk < own_blk, jnp.sum(open_prod / MOBA_BLOCK), jnp.float32(NEG))
            for j in range(topk):
                def scan(n, carry, h=h):
                    best, bi = carry
                    g = gate_s[h * gs + n]
                    better = g > best
                    return jnp.where(better, g, best), jnp.where(better, n, bi)
                _, bi = lax.fori_loop(1, gs, scan, (gate_s[h * gs], jnp.int32(0)))
                gate_s[h * gs + bi] = jnp.float32(-jnp.inf)
                sel_s[h * topk + j] = bi

        def sel_page(h, jr):
            return sel_s[h * topk + jr // per_blk] * per_blk + jr % per_blk

        def rows_copy(src_hbm, slot):
            return pltpu.make_async_copy(src_hbm.at[idx_v.at[slot]], rowbuf.at[slot], rsem.at[slot])

        def rows_start(src_hbm, h, pg, slot):
            first = phys_page(pg) * page
            for c in range(groups):
                idx_v[slot, pl.ds(c * L, L)] = (first + c * L + lane) * heads + h
            rows_copy(src_hbm, slot).start()

        @pl.loop(0, heads)
        def _(h):
            rows_start(kr_hbm, h, sel_page(h, 0), 0)
            for jr in range(n_sel_pages):
                slot = jr % 2
                rows_copy(kr_hbm, slot).wait()
                if jr + 1 < n_sel_pages:
                    rows_start(kr_hbm, h, sel_page(h, jr + 1), 1 - slot)
                else:
                    rows_start(vr_hbm, h, sel_page(h, 0), 1 - slot)

                def dots(d, accs, slot=slot):
                    col = (d + lane) & (hd - 1)
                    qd = plsc.load_gather(q_v, [h * hd + col])
                    return tuple(accs[g] + plsc.load_gather(rowbuf.at[slot], [g * L + lane, col]) * qd
                                 for g in range(groups))
                accs = lax.fori_loop(0, hd, dots, tuple(jnp.zeros((L,), F32) for _ in range(groups)))
                for g in range(groups):
                    s_v[pl.ds(jr * page + g * L, L)] = accs[g] * scale
            own = q_v[pl.ds(h * hd, L)] * kn_v[pl.ds(h * hd, L)]
            for c in range(1, hv):
                own = own + q_v[pl.ds(h * hd + c * L, L)] * kn_v[pl.ds(h * hd + c * L, L)]
            s_own = jnp.sum(own) * scale
            n_vec = n_sel_pages * groups

            def vmax(i, m):
                return jnp.maximum(m, s_v[pl.ds(i * L, L)])
            mx = jnp.maximum(jnp.max(lax.fori_loop(0, n_vec, vmax, jnp.full((L,), NEG, F32))), s_own)

            def expsum(i, tot):
                p = jnp.exp(s_v[pl.ds(i * L, L)] - mx)
                s_v[pl.ds(i * L, L)] = p
                return tot + p
            p_own = jnp.max(jnp.exp(jnp.full((L,), s_own - mx, F32)))
            den = jnp.sum(lax.fori_loop(0, n_vec, expsum, jnp.zeros((L,), F32))) + p_own
            outs = tuple(p_own * vn_v[pl.ds(h * hd + c * L, L)] for c in range(hv))
            v_first = n_sel_pages % 2
            for jr in range(n_sel_pages):
                slot = (v_first + jr) % 2
                rows_copy(vr_hbm, slot).wait()
                if jr + 1 < n_sel_pages:
                    rows_start(vr_hbm, h, sel_page(h, jr + 1), 1 - slot)

                def pv(t, o, slot=slot, jr=jr):
                    p = plsc.load_gather(s_v, [jnp.full((L,), jr * page, jnp.int32) + t])
                    return tuple(o[c] + p * rowbuf[slot, t, pl.ds(c * L, L)] for c in range(hv))
                outs = lax.fori_loop(0, page, pv, outs)
            for c in range(hv):
                out_v[pl.ds(h * hd + c * L, L)] = outs[c] / den
        pltpu.sync_copy(out_v, o_hbm.at[b])

    return pl.kernel(
        kern,
        out_type=jax.ShapeDtypeStruct((bsz, width), F32),
        mesh=mesh,
        scratch_types=[pltpu.VMEM((2, SC_CHUNK_ROWS, hd), F32),
                       pltpu.VMEM((heads, hd), F32),
                       pltpu.VMEM((width,), F32),
                       pltpu.VMEM((width,), F32),
                       pltpu.VMEM((width,), F32),
                       pltpu.VMEM((n_pages,), jnp.int32),
                       pltpu.VMEM((2, page, hd), F32),
                       pltpu.VMEM((2, page), jnp.int32),
                       pltpu.VMEM((n_sel_pages * page,), F32),
                       pltpu.VMEM((width,), F32),
                       pltpu.SMEM((heads * gs,), F32),
                       pltpu.SMEM((heads * topk,), jnp.int32),
                       pltpu.SemaphoreType.DMA((2,)),
                       pltpu.SemaphoreType.DMA((2,))],
        compiler_params=pltpu.CompilerParams(needs_layout_passes=False),
        name="moba_sample",
    )(krows, vrows, page_table, q, k_new, v_new)


def _mem_sample_kernel(q_ref, k_ref, v_ref, o_ref):
    hd = q_ref.shape[-1]
    q = q_ref[0] * (hd ** -0.5)
    s = jnp.sum(k_ref[0, 0] * q, axis=-1, keepdims=True)
    mx = jnp.max(s, axis=0, keepdims=True)
    p = jnp.exp(s - mx)
    den = jnp.sum(p, axis=0)
    o = jnp.sum(p * v_ref[0, 0], axis=0)
    o_ref[0] = o / den


def _mem_sample(q, mem_k, mem_v):
    _, bsz, n_mem, heads, hd = mem_k.shape
    tok = pl.BlockSpec((1, heads, hd), lambda b: (b, 0, 0))
    mem = pl.BlockSpec((1, 1, n_mem, heads, hd), lambda b: (0, b, 0, 0, 0))
    return pl.pallas_call(
        _mem_sample_kernel,
        out_shape=jax.ShapeDtypeStruct((bsz, heads, hd), F32),
        grid=(bsz,),
        in_specs=[tok, mem, mem],
        out_specs=tok,
        compiler_params=_params("arbitrary"),
        name="mem_sample",
    )(q, mem_k, mem_v)


def _pool_sample_kernel(u_ref, st_ref, w_ref, sc_ref, o_ref, *, windows):
    ctx = st_ref.shape[0]
    gc = w_ref.shape[-1]
    for g, w in enumerate(windows):
        sl = slice(g * gc, (g + 1) * gc)
        x = u_ref[:, sl]
        win = x
        for j in range(1, w):
            win = win + st_ref[ctx - j, :, sl]
        mix = win / float(w) - x
        y = jnp.dot(mix.astype(BF16), w_ref[g], preferred_element_type=F32)
        o_ref[:, sl] = (y * sc_ref[:, sl]).astype(o_ref.dtype)


def _pool_sample(u, state, w_pool, scale):
    bsz, width = u.shape
    return pl.pallas_call(
        functools.partial(_pool_sample_kernel, windows=POOL_WINDOWS),
        out_shape=jax.ShapeDtypeStruct((bsz, width), BF16),
        name="pool_sample",
        compiler_params=pltpu.CompilerParams(vmem_limit_bytes=V7X_VMEM_LIMIT_BYTES),
    )(u, state, w_pool, scale)


def kernel(x_prompt, x_sample, cache_k, cache_v, cache_mem_k, cache_mem_v, state_pool, page_table, mem_prompt, w_in, w_mem_kv, w_pool, pool_scale, w_br_attn, w_br_pool, w_br_mem, w_o, ln1_g, ln1_b, w_up, w_down, ln2_g, ln2_b):
    depth = w_in.shape[0]
    assert depth == 1, "single-layer trunk"
    batch, seq, d_model = x_prompt.shape
    dec_batch, dec_seq, _ = x_sample.shape
    assert dec_seq == 1
    _, n_phys, page, a_heads, a_hd = cache_k.shape
    assert page == PAGE_ROWS
    _, _, n_mem, m_heads, m_hd = cache_mem_k.shape
    pool_ctx, pool_width = state_pool.shape[2], state_pool.shape[3]
    a_width = a_heads * a_hd
    m_width = m_heads * m_hd
    alpha = float((2 * depth) ** 0.25)
    assert a_width == pool_width == m_width == COL_TILE
    k_tile, v_tile = 1, 2
    n_tiles = w_in.shape[2] // COL_TILE

    w_pool_b = w_pool[0].astype(BF16)

    xs = x_sample.reshape(dec_batch, d_model)
    p3s = _project(xs, w_in[0], lambda j: j, n_tiles, tm=dec_batch, name="in_proj_sample")[0]
    q_s, k_s, v_s, u_s, mq_s = p3s[0], p3s[k_tile], p3s[v_tile], p3s[3], p3s[4]

    mp = batch * seq
    xp = x_prompt.reshape(mp, d_model)
    u_col, mq_col, gate_col = 1, 2, 3
    k3, k_heads, xb = _project(xp, w_in[0], lambda j: k_tile, 1, tm=1024, name="k_proj_prompt",
                               heads=a_heads, emit_x=True)
    v3, v_heads = _project(xb, w_in[0], lambda j: v_tile, 1, tm=1024, name="v_proj_prompt", heads=a_heads)
    p3 = _project(xb, w_in[0], lambda j: jnp.where(j == 0, 0, j + 2), n_tiles - 2,
                  tm=1024, name="in_proj_prompt")[0]
    attn_p, (w_up_b, w_down_b, w_o_b) = _moba_prompt(
        p3, k3[0], v3[0], batch=batch, seq=seq, heads=a_heads, hd=a_hd, q_col=0,
        cast=(w_up[0], w_down[0], w_o[0]))
    pool_p = _pool_prompt(p3, w_pool_b, pool_scale, batch=batch, seq=seq, u_col=u_col)
    kv3 = _project(mem_prompt.reshape(batch * n_mem, d_model), w_mem_kv[0], lambda j: j, 2,
                   tm=batch * n_mem, name="mem_kv_prompt")[0]
    memo_p = _mem_prompt(p3, kv3, batch=batch, seq=seq, n_mem=n_mem, heads=m_heads, hd=m_hd, q_col=mq_col)
    mix_p = _merge(attn_p, pool_p, memo_p, w_br_attn[0], w_br_pool[0], w_br_mem[0], p3,
                   gate_col=gate_col, tm=512)
    h_p = _oproj(xp, mix_p, w_o_b, ln1_g, ln1_b, alpha=alpha, tm=512)
    y_p = _ffn(h_p, w_up_b, w_down_b, ln2_g, ln2_b, alpha=alpha, tm=1024, tk=512)

    memo_s = _mem_sample(mq_s.reshape(dec_batch, m_heads, m_hd), cache_mem_k, cache_mem_v)
    pool_s = _pool_sample(u_s, jnp.transpose(state_pool[0], (1, 0, 2)), w_pool_b, pool_scale)
    attn_s = _moba_sample(cache_k, cache_v, page_table, q_s, k_s, v_s)
    mix_s = _merge(attn_s, pool_s, memo_s.reshape(dec_batch, m_width), w_br_attn[0], w_br_pool[0], w_br_mem[0],
                   p3s, gate_col=5, tm=dec_batch)
    h_s = _oproj(xs, mix_s, w_o_b, ln1_g, ln1_b, alpha=alpha, tm=dec_batch)
    y_s = _ffn(h_s, w_up_b, w_down_b, ln2_g, ln2_b, alpha=alpha, tm=dec_batch, tk=2048)

    kv_shape = (depth, batch, seq, a_heads, a_hd)
    mem_shape = (depth, batch, n_mem, m_heads, m_hd)
    new_pool_p = p3.reshape(-1, batch, seq, pool_width)[u_col, :, seq - pool_ctx:, :][None]
    new_pool_s = jnp.concatenate([state_pool[0][:, 1:, :], u_s[:, None, :]], axis=1)[None]
    skv_shape = (depth, dec_batch, dec_seq, a_heads, a_hd)
    return (y_p.reshape(batch, seq, d_model), y_s.reshape(dec_batch, dec_seq, d_model),
            k_heads.reshape(kv_shape), v_heads.reshape(kv_shape),
            kv3[0].reshape(mem_shape), kv3[1].reshape(mem_shape),
            new_pool_p,
            k_s.reshape(skv_shape), v_s.reshape(skv_shape),
            new_pool_s)
```

```python
import functools

import jax
import jax.numpy as jnp
from jax import lax
from jax.experimental import pallas as pl
from jax.experimental.pallas import tpu as pltpu
from jax.experimental.pallas import tpu_sc as plsc

MOBA_BLOCK = 256
MOBA_TOPK = 3
POOL_WINDOWS = (2, 4, 8, 16)
N_BRANCH = 3
LN_EPS = 1e-5
NEG = -1e30
PAGE_ROWS = 128

V7X_VMEM_LIMIT_BYTES = 60000 * 1024
V7X_SC_LANES = 16
COL_TILE = 1024
FFN_SLAB = 512
CAST_PARTS = 4
BF16_ROWS = 16
SC_CHUNK_ROWS = 256

BF16 = jnp.bfloat16
F32 = jnp.float32
NT = (((1,), (1,)), ((), ()))


def _params(*semantics):
    return pltpu.CompilerParams(dimension_semantics=semantics,
                                vmem_limit_bytes=V7X_VMEM_LIMIT_BYTES)


def _layer_norm(x, g, b):
    mu = jnp.mean(x, axis=-1, keepdims=True)
    xc = x - mu
    var = jnp.mean(xc * xc, axis=-1, keepdims=True)
    return xc * lax.rsqrt(var + LN_EPS) * g + b


def _proj_kernel(x_ref, w_ref, o_ref, *rest, heads, emit_x):
    wb_ref = rest[-1]

    @pl.when(pl.program_id(1) == 0)
    def _():
        wb_ref[...] = w_ref[...].astype(BF16)

    xb = x_ref[...].astype(BF16)
    y = jnp.dot(xb, wb_ref[...], preferred_element_type=F32)
    o_ref[0] = y
    if heads:
        oh_ref = rest[0]
        tm, hd = y.shape[0], y.shape[1] // heads
        for h in range(heads):
            oh_ref[pl.ds(h, tm, stride=heads), :] = y[:, h * hd:(h + 1) * hd]
    if emit_x:
        rest[-2][...] = xb


def _project(x, w, col_of, n_cols, *, tm, name, heads=0, emit_x=False):
    m, k = x.shape
    tn = COL_TILE
    out_shape = [jax.ShapeDtypeStruct((n_cols, m, tn), F32)]
    out_specs = [pl.BlockSpec((1, tm, tn), lambda j, i: (j, i, 0))]
    if heads:
        assert n_cols == 1
        out_shape.append(jax.ShapeDtypeStruct((m * heads, tn // heads), F32))
        out_specs.append(pl.BlockSpec((tm * heads, tn // heads), lambda j, i: (i, 0)))
    if emit_x:
        assert n_cols == 1
        out_shape.append(jax.ShapeDtypeStruct((m, k), BF16))
        out_specs.append(pl.BlockSpec((tm, k), lambda j, i: (i, 0)))
    return pl.pallas_call(
        functools.partial(_proj_kernel, heads=heads, emit_x=emit_x),
        out_shape=out_shape,
        grid=(n_cols, m // tm),
        in_specs=[pl.BlockSpec((tm, k), lambda j, i: (i, 0)),
                  pl.BlockSpec((k, tn), lambda j, i: (0, col_of(j)))],
        out_specs=out_specs,
        scratch_shapes=[pltpu.VMEM((k, tn), BF16)],
        compiler_params=_params("arbitrary", "arbitrary"),
        name=name,
    )(x, w)


def _moba_prompt_kernel(q_ref, k_ref, v_ref, *rest, seq, blk, topk, cast_blocks):
    n_src = sum(cast_blocks)
    cast_src, o_ref, cast_dst = rest[:n_src], rest[n_src], rest[n_src + 1:]
    nb = seq // blk
    hd = q_ref.shape[-1]
    q = q_ref[0]
    k = k_ref[...]
    means = jnp.concatenate(
        [jnp.mean(k[n * blk:(n + 1) * blk], axis=0, keepdims=True) for n in range(nb)], axis=0)
    qs = q * (hd ** -0.5)
    kb = k.astype(BF16)
    vb = v_ref[...].astype(BF16)
    key_blk = lax.broadcasted_iota(jnp.int32, (seq, hd), 0) // blk
    lane = lax.broadcasted_iota(jnp.int32, (seq, hd), 1)
    k_ext = jnp.concatenate([kb, jnp.where(key_blk == lane, 1.0, 0.0).astype(BF16)], axis=1)
    row = lax.broadcasted_iota(jnp.int32, (blk, blk), 0)
    col = lax.broadcasted_iota(jnp.int32, (blk, blk), 1)
    causal = col <= row
    blk_id = lax.broadcasted_iota(jnp.int32, (nb, blk), 0)
    for i in range(nb):
        qi = qs[i * blk:(i + 1) * blk]
        if i > topk:
            gate = lax.dot_general(means, q[i * blk:(i + 1) * blk], NT,
                                   precision=lax.Precision.HIGHEST, preferred_element_type=F32)
            beaten = jnp.zeros((nb, blk), jnp.int32)
            for m in range(i):
                gm = gate[m:m + 1, :]
                beaten = beaten + jnp.where(m < blk_id, jnp.where(gm >= gate, 1, 0), jnp.where(gm > gate, 1, 0))
            bias = jnp.where((beaten < topk) | (blk_id >= i), 0.0, NEG)
            bias = jnp.concatenate([bias, jnp.zeros((hd - nb, blk), F32)], axis=0).T
            q_ext = jnp.concatenate([qi.astype(BF16), bias.astype(BF16)], axis=1)
            s = lax.dot_general(q_ext, k_ext[:(i + 1) * blk], NT, preferred_element_type=F32)
        else:
            s = lax.dot_general(qi.astype(BF16), kb[:(i + 1) * blk], NT, preferred_element_type=F32)
        own = jnp.where(causal, s[:, i * blk:(i + 1) * blk], NEG)
        s = jnp.concatenate([s[:, :i * blk], own], axis=1) if i else own
        mx = jnp.max(s, axis=-1, keepdims=True)
        p = jnp.exp(s - mx)
        den = jnp.sum(p, axis=-1, keepdims=True)
        o = jnp.dot(p.astype(BF16), vb[:(i + 1) * blk], preferred_element_type=F32)
        o_ref[i * blk:(i + 1) * blk, :] = (o / den).astype(o_ref.dtype)
        first = 0
        for dst, n_blocks in zip(cast_dst, cast_blocks):
            parts = min(CAST_PARTS, dst.shape[0] // BF16_ROWS)
            every = nb // parts
            if i % every == every - 1:
                rows = dst.shape[0] // parts
                rsl = slice((i // every) * rows, (i // every + 1) * rows)
                for c, src in enumerate(cast_src[first:first + n_blocks]):
                    w = src.shape[1]
                    dst[rsl, c * w:(c + 1) * w] = src[rsl, :].astype(BF16)
            first += n_blocks


def _moba_prompt(p3, k2, v2, *, batch, seq, heads, hd, q_col, cast=()):
    assert heads * hd == COL_TILE
    steps = batch * heads
    kv = pl.BlockSpec((seq, hd), lambda b, h: (b, h))
    srcs, src_specs, dst_shapes, dst_specs, cast_blocks = [], [], [], [], []
    for entry in cast:
        w, col0, cols = entry if isinstance(entry, tuple) else (entry, 0, entry.shape[1])
        rows = w.shape[0] // steps
        assert w.shape[0] % (steps * BF16_ROWS) == 0
        if col0 == 0 and cols == w.shape[1]:
            blocks = [(cols, 0)]
        else:
            assert col0 % COL_TILE == 0 and cols % COL_TILE == 0
            blocks = [(COL_TILE, col0 // COL_TILE + c) for c in range(cols // COL_TILE)]
        for width, cb in blocks:
            srcs.append(w)
            src_specs.append(pl.BlockSpec((rows, width), lambda b, h, cb=cb: (b * heads + h, cb)))
        cast_blocks.append(len(blocks))
        dst_shapes.append(jax.ShapeDtypeStruct((w.shape[0], cols), BF16))
        dst_specs.append(pl.BlockSpec((rows, cols), lambda b, h: (b * heads + h, 0)))
    kern = functools.partial(_moba_prompt_kernel, seq=seq, blk=MOBA_BLOCK, topk=MOBA_TOPK,
                             cast_blocks=tuple(cast_blocks))
    out = pl.pallas_call(
        kern,
        out_shape=[jax.ShapeDtypeStruct((batch * seq, heads * hd), BF16)] + dst_shapes,
        grid=(batch, heads),
        in_specs=[pl.BlockSpec((1, seq, hd), lambda b, h: (q_col, b, h)), kv, kv] + src_specs,
        out_specs=[kv] + dst_specs,
        compiler_params=_params("arbitrary", "arbitrary"),
        name="moba_prompt",
    )(p3, k2, v2, *srcs)
    return out[0], out[1:]


def _pool_prompt_kernel(u_ref, w_ref, sc_ref, o_ref, *, windows):
    seq = u_ref.shape[1]
    gc = w_ref.shape[-1]
    t = lax.broadcasted_iota(jnp.int32, (seq, 1), 0)
    for g, w in enumerate(windows):
        x = u_ref[0, :, g * gc:(g + 1) * gc]
        win = x
        width = 1
        while width < w:
            shifted = pltpu.roll(win, width, 0)
            win = win + jnp.where(t >= width, shifted, 0.0)
            width *= 2
        cnt = jnp.minimum(w, t + 1).astype(F32)
        mix = win / cnt - x
        y = jnp.dot(mix.astype(BF16), w_ref[g], preferred_element_type=F32)
        o_ref[:, g * gc:(g + 1) * gc] = (y * sc_ref[:, g * gc:(g + 1) * gc]).astype(o_ref.dtype)


def _pool_prompt(p3, w_pool, scale, *, batch, seq, u_col):
    groups, gc, _ = w_pool.shape
    width = groups * gc
    assert width == COL_TILE
    kern = functools.partial(_pool_prompt_kernel, windows=POOL_WINDOWS)
    return pl.pallas_call(
        kern,
        out_shape=jax.ShapeDtypeStruct((batch * seq, width), BF16),
        grid=(batch,),
        in_specs=[pl.BlockSpec((1, seq, width), lambda b: (u_col, b, 0)),
                  pl.BlockSpec((groups, gc, gc), lambda b: (0, 0, 0)),
                  pl.BlockSpec((1, width), lambda b: (0, 0))],
        out_specs=pl.BlockSpec((seq, width), lambda b: (b, 0)),
        compiler_params=_params("arbitrary"),
        name="pool_prompt",
    )(p3, w_pool, scale)


def _mem_prompt_kernel(q_ref, k_ref, v_ref, o_ref):
    hd = q_ref.shape[-1]
    qb = (q_ref[0] * (hd ** -0.5)).astype(BF16)
    s = lax.dot_general(qb, k_ref[0].astype(BF16), NT, preferred_element_type=F32)
    mx = jnp.max(s, axis=-1, keepdims=True)
    p = jnp.exp(s - mx)
    den = jnp.sum(p, axis=-1, keepdims=True)
    o = jnp.dot(p.astype(BF16), v_ref[0].astype(BF16), preferred_element_type=F32)
    o_ref[...] = (o / den).astype(o_ref.dtype)


def _mem_prompt(p3, kv3, *, batch, seq, n_mem, heads, hd, q_col):
    assert heads * hd == COL_TILE
    return pl.pallas_call(
        _mem_prompt_kernel,
        out_shape=jax.ShapeDtypeStruct((batch * seq, heads * hd), BF16),
        grid=(batch, heads),
        in_specs=[pl.BlockSpec((1, seq, hd), lambda b, h: (q_col, b, h)),
                  pl.BlockSpec((1, n_mem, hd), lambda b, h: (0, b, h)),
                  pl.BlockSpec((1, n_mem, hd), lambda b, h: (1, b, h))],
        out_specs=pl.BlockSpec((seq, hd), lambda b, h: (b, h)),
        compiler_params=_params("arbitrary", "arbitrary"),
        name="mem_prompt",
    )(p3, kv3, kv3)


def _merge_kernel(x_ref, a_ref, p_ref, m_ref, wa_ref, wp_ref, wm_ref, ga_ref, gp_ref, gm_ref, o_ref):
    xb = x_ref[...].astype(BF16)

    def branch(b_ref, w_ref, g_ref):
        gate = jnp.dot(xb, g_ref[...], preferred_element_type=F32)
        y = jnp.dot(b_ref[...].astype(BF16), w_ref[...], preferred_element_type=F32)
        return jax.nn.sigmoid(gate) * y
    mix = branch(a_ref, wa_ref, ga_ref) + branch(p_ref, wp_ref, gp_ref) + branch(m_ref, wm_ref, gm_ref)
    o_ref[...] = mix.astype(o_ref.dtype)


def _merge(x, attn, pool, memo, w_ba, w_bp, w_bm, w_gate, *, tm):
    m, kw = attn.shape
    d = w_ba.shape[1]
    tn = COL_TILE
    per_gate = d // tn
    row = lambda w: pl.BlockSpec((tm, w), lambda j, i: (i, 0))
    wsp = pl.BlockSpec((kw, tn), lambda j, i: (0, j))
    gsp = lambda c: pl.BlockSpec((d, tn), lambda j, i: (0, c * per_gate + j))
    return pl.pallas_call(
        _merge_kernel,
        out_shape=jax.ShapeDtypeStruct((m, d), BF16),
        grid=(d // tn, m // tm),
        in_specs=[row(d), row(kw), row(kw), row(kw), wsp, wsp, wsp, gsp(0), gsp(1), gsp(2)],
        out_specs=pl.BlockSpec((tm, tn), lambda j, i: (i, j)),
        compiler_params=_params("arbitrary", "arbitrary"),
        name="merge",
    )(x, attn, pool, memo, w_ba, w_bp, w_bm, w_gate, w_gate, w_gate)


def _oproj_kernel(x_ref, mix_ref, w_ref, g_ref, b_ref, o_ref, *, alpha):
    y = jnp.dot(mix_ref[...], w_ref[...], preferred_element_type=F32)
    o_ref[...] = _layer_norm(alpha * x_ref[...] + y, g_ref[...], b_ref[...])


def _oproj(x, mix, w_o, g, b, *, alpha, tm):
    m, d = x.shape
    row = pl.BlockSpec((tm, d), lambda i: (i, 0))
    vec = pl.BlockSpec((1, d), lambda i: (0, 0))
    return pl.pallas_call(
        functools.partial(_oproj_kernel, alpha=alpha),
        out_shape=jax.ShapeDtypeStruct((m, d), F32),
        grid=(m // tm,),
        in_specs=[row, row, pl.BlockSpec((d, d), lambda i: (0, 0)), vec, vec],
        out_specs=row,
        compiler_params=_params("arbitrary"),
        name="oproj_ln1",
    )(x, mix, w_o, g, b)


def _ffn_kernel(h_ref, wu_ref, wd_ref, g_ref, b_ref, o_ref, hb_ref, *, alpha):
    kk = pl.program_id(1)

    @pl.when(kk == 0)
    def _():
        hb_ref[...] = h_ref[...].astype(BF16)
        o_ref[...] = jnp.zeros_like(o_ref)

    a = jnp.dot(hb_ref[...], wu_ref[...], preferred_element_type=F32)
    a = jnp.square(jnp.maximum(a, 0.0)).astype(BF16)
    d = o_ref.shape[1]
    for c in range(0, d, FFN_SLAB):
        o_ref[:, c:c + FFN_SLAB] += jnp.dot(a, wd_ref[:, c:c + FFN_SLAB], preferred_element_type=F32)

    @pl.when(kk == pl.num_programs(1) - 1)
    def _():
        o_ref[...] = _layer_norm(alpha * h_ref[...] + o_ref[...], g_ref[...], b_ref[...])


def _ffn(h, w_up, w_down, g, b, *, alpha, tm, tk):
    m, d = h.shape
    dff = w_up.shape[1]
    row = pl.BlockSpec((tm, d), lambda i, k: (i, 0))
    vec = pl.BlockSpec((1, d), lambda i, k: (0, 0))
    return pl.pallas_call(
        functools.partial(_ffn_kernel, alpha=alpha),
        out_shape=jax.ShapeDtypeStruct((m, d), F32),
        grid=(m // tm, dff // tk),
        in_specs=[row,
                  pl.BlockSpec((d, tk), lambda i, k: (0, k)),
                  pl.BlockSpec((tk, d), lambda i, k: (k, 0)),
                  vec, vec],
        out_specs=row,
        scratch_shapes=[pltpu.VMEM((tm, d), BF16)],
        compiler_params=_params("arbitrary", "arbitrary"),
        name="ffn_ln2",
    )(h, w_up, w_down, g, b)


def _moba_sample(cache_k, cache_v, page_table, q, k_new, v_new):
    _, n_phys, page, heads, hd = cache_k.shape
    bsz, n_pages = page_table.shape
    L = V7X_SC_LANES
    topk = MOBA_TOPK
    per_blk = MOBA_BLOCK // page
    n_blk = n_pages // per_blk
    rows_per_page = page * heads
    n_ch = rows_per_page // SC_CHUNK_ROWS
    items_per_blk = per_blk * n_ch
    n_items = n_pages * n_ch
    width = heads * hd
    hv = hd // L
    groups = page // L
    n_sel_pages = topk * per_blk
    gs = n_blk + 1
    scale = hd ** -0.5
    assert hd & (hd - 1) == 0 and n_pages % L == 0 and SC_CHUNK_ROWS % heads == 0
    krows = cache_k.reshape(n_phys * rows_per_page, hd)
    vrows = cache_v.reshape(n_phys * rows_per_page, hd)
    mesh = plsc.VectorSubcoreMesh(core_axis_name="core", subcore_axis_name="subcore")
    assert bsz == mesh.num_cores * mesh.num_subcores, "one batch row per vector subcore"

    def kern(kr_hbm, vr_hbm, pt_hbm, q_hbm, kn_hbm, vn_hbm, o_hbm,
             buf, acc, q_v, kn_v, vn_v, pt_v, rowbuf, idx_v, s_v, out_v, gate_s, sel_s, sems, rsem):
        b = lax.axis_index("core") * mesh.num_subcores + lax.axis_index("subcore")
        pltpu.sync_copy(q_hbm.at[b], q_v)
        pltpu.sync_copy(kn_hbm.at[b], kn_v)
        pltpu.sync_copy(vn_hbm.at[b], vn_v)
        pltpu.sync_copy(pt_hbm.at[b], pt_v)
        lane = lax.iota(jnp.int32, L)

        def phys_page(pg):
            group = pt_v[pl.ds((pg // L) * L, L)]
            return jnp.sum(jnp.where(lane == pg % L, group, 0))

        def chunk_start(item, slot):
            start = phys_page(item // n_ch) * rows_per_page + (item % n_ch) * SC_CHUNK_ROWS
            pltpu.make_async_copy(kr_hbm.at[pl.ds(pl.multiple_of(start, SC_CHUNK_ROWS), SC_CHUNK_ROWS)],
                                  buf.at[slot], sems.at[slot]).start()

        def chunk_wait(slot):
            pltpu.make_async_copy(kr_hbm.at[pl.ds(0, SC_CHUNK_ROWS)], buf.at[slot], sems.at[slot]).wait()

        chunk_start(0, 0)

        @pl.loop(0, n_blk)
        def _(n):
            for h in range(heads):
                for c in range(hv):
                    acc[h, pl.ds(c * L, L)] = jnp.zeros((L,), F32)
            for j in range(items_per_blk):
                slot = j % 2
                item = n * items_per_blk + j
                chunk_wait(slot)

                @pl.when(item + 1 < n_items)
                def _():
                    chunk_start(item + 1, 1 - slot)

                @pl.loop(0, heads * hv)
                def _(hc):
                    h = hc // hv
                    cols = pl.ds((hc % hv) * L, L)
                    rows = [t * heads + h for t in range(SC_CHUNK_ROWS // heads)]
                    parts = [buf[slot, r, cols] for r in rows[:4]]
                    for i, r in enumerate(rows[4:]):
                        parts[i % 4] = parts[i % 4] + buf[slot, r, cols]
                    plsc.addupdate(acc.at[h, cols], (parts[0] + parts[1]) + (parts[2] + parts[3]))
            for h in range(heads):
                prod = acc[h, pl.ds(0, L)] * q_v[pl.ds(h * hd, L)]
                for c in range(1, hv):
                    prod = prod + acc[h, pl.ds(c * L, L)] * q_v[pl.ds(h * hd + c * L, L)]
                gate_s[h * gs + n] = jnp.sum(prod / MOBA_BLOCK)

        own_blk = n_blk
        for h in range(heads):
            open_prod = kn_v[pl.ds(h * hd, L)] * q_v[pl.ds(h * hd, L)]
            for c in range(1, hv):
                open_prod = open_prod + kn_v[pl.ds(h * hd + c * L, L)] * q_v[pl.ds(h * hd + c * L, L)]
            gate_s[h * gs + n_blk] = jnp.where(n_blk < own_blk, jnp.sum(open_prod / MOBA_BLOCK), jnp.float32(NEG))
            for j in range(topk):
                def scan(n, carry, h=h):
                    best, bi = carry
                    g = gate_s[h * gs + n]
                    better = g > best
                    return jnp.where(better, g, best), jnp.where(better, n, bi)
                _, bi = lax.fori_loop(1, gs, scan, (gate_s[h * gs], jnp.int32(0)))
                gate_s[h * gs + bi] = jnp.float32(-jnp.inf)
                sel_s[h * topk + j] = bi

        def sel_page(h, jr):
            return sel_s[h * topk + jr // per_blk] * per_blk + jr % per_blk

        def rows_copy(src_hbm, slot):
            return pltpu.make_async_copy(src_hbm.at[idx_v.at[slot]], rowbuf.at[slot], rsem.at[slot])

        def rows_start(src_hbm, h, pg, slot):
            first = phys_page(pg) * page
            for c in range(groups):
                idx_v[slot, pl.ds(c * L, L)] = (first + c * L + lane) * heads + h
            rows_copy(src_hbm, slot).start()

        @pl.loop(0, heads)
        def _(h):
            rows_start(kr_hbm, h, sel_page(h, 0), 0)
            for jr in range(n_sel_pages):
                slot = jr % 2
                rows_copy(kr_hbm, slot).wait()
                if jr + 1 < n_sel_pages:
                    rows_start(kr_hbm, h, sel_page(h, jr + 1), 1 - slot)
                else:
                    rows_start(vr_hbm, h, sel_page(h, 0), 1 - slot)

                def dots(d, accs, slot=slot):
                    col = (d + lane) & (hd - 1)
                    qd = plsc.load_gather(q_v, [h * hd + col])
                    return tuple(accs[g] + plsc.load_gather(rowbuf.at[slot], [g * L + lane, col]) * qd
                                 for g in range(groups))
                accs = lax.fori_loop(0, hd, dots, tuple(jnp.zeros((L,), F32) for _ in range(groups)))
                for g in range(groups):
                    s_v[pl.ds(jr * page + g * L, L)] = accs[g] * scale
            own = q_v[pl.ds(h * hd, L)] * kn_v[pl.ds(h * hd, L)]
            for c in range(1, hv):
                own = own + q_v[pl.ds(h * hd + c * L, L)] * kn_v[pl.ds(h * hd + c * L, L)]
            s_own = jnp.sum(own) * scale
            n_vec = n_sel_pages * groups

            def vmax(i, m):
                return jnp.maximum(m, s_v[pl.ds(i * L, L)])
            mx = jnp.maximum(jnp.max(lax.fori_loop(0, n_vec, vmax, jnp.full((L,), NEG, F32))), s_own)

            def expsum(i, tot):
                p = jnp.exp(s_v[pl.ds(i * L, L)] - mx)
                s_v[pl.ds(i * L, L)] = p
                return tot + p
            p_own = jnp.max(jnp.exp(jnp.full((L,), s_own - mx, F32)))
            den = jnp.sum(lax.fori_loop(0, n_vec, expsum, jnp.zeros((L,), F32))) + p_own
            outs = tuple(p_own * vn_v[pl.ds(h * hd + c * L, L)] for c in range(hv))
            v_first = n_sel_pages % 2
            for jr in range(n_sel_pages):
                slot = (v_first + jr) % 2
                rows_copy(vr_hbm, slot).wait()
                if jr + 1 < n_sel_pages:
                    rows_start(vr_hbm, h, sel_page(h, jr + 1), 1 - slot)

                def pv(t, o, slot=slot, jr=jr):
                    p = plsc.load_gather(s_v, [jnp.full((L,), jr * page, jnp.int32) + t])
                    return tuple(o[c] + p * rowbuf[slot, t, pl.ds(c * L, L)] for c in range(hv))
                outs = lax.fori_loop(0, page, pv, outs)
            for c in range(hv):
                out_v[pl.ds(h * hd + c * L, L)] = outs[c] / den
        pltpu.sync_copy(out_v, o_hbm.at[b])

    return pl.kernel(
        kern,
        out_type=jax.ShapeDtypeStruct((bsz, width), F32),
        mesh=mesh,
        scratch_types=[pltpu.VMEM((2, SC_CHUNK_ROWS, hd), F32),
                       pltpu.VMEM((heads, hd), F32),
                       pltpu.VMEM((width,), F32),
                       pltpu.VMEM((width,), F32),
                       pltpu.VMEM((width,), F32),
                       pltpu.VMEM((n_pages,), jnp.int32),
                       pltpu.VMEM((2, page, hd), F32),
                       pltpu.VMEM((2, page), jnp.int32),
                       pltpu.VMEM((n_sel_pages * page,), F32),
                       pltpu.VMEM((width,), F32),
                       pltpu.SMEM((heads * gs,), F32),
                       pltpu.SMEM((heads * topk,), jnp.int32),
                       pltpu.SemaphoreType.DMA((2,)),
                       pltpu.SemaphoreType.DMA((2,))],
        compiler_params=pltpu.CompilerParams(needs_layout_passes=False),
        name="moba_sample",
    )(krows, vrows, page_table, q, k_new, v_new)


def _mem_sample_kernel(q_ref, k_ref, v_ref, o_ref):
    hd = q_ref.shape[-1]
    q = q_ref[0] * (hd ** -0.5)
    s = jnp.sum(k_ref[0, 0] * q, axis=-1, keepdims=True)
    mx = jnp.max(s, axis=0, keepdims=True)
    p = jnp.exp(s - mx)
    den = jnp.sum(p, axis=0)
    o = jnp.sum(p * v_ref[0, 0], axis=0)
    o_ref[0] = o / den


def _mem_sample(q, mem_k, mem_v):
    _, bsz, n_mem, heads, hd = mem_k.shape
    tok = pl.BlockSpec((1, heads, hd), lambda b: (b, 0, 0))
    mem = pl.BlockSpec((1, 1, n_mem, heads, hd), lambda b: (0, b, 0, 0, 0))
    return pl.pallas_call(
        _mem_sample_kernel,
        out_shape=jax.ShapeDtypeStruct((bsz, heads, hd), F32),
        grid=(bsz,),
        in_specs=[tok, mem, mem],
        out_specs=tok,
        compiler_params=_params("arbitrary"),
        name="mem_sample",
    )(q, mem_k, mem_v)


def _pool_sample_kernel(u_ref, st_ref, w_ref, sc_ref, o_ref, *, windows):
    ctx = st_ref.shape[0]
    gc = w_ref.shape[-1]
    for g, w in enumerate(windows):
        sl = slice(g * gc, (g + 1) * gc)
        x = u_ref[:, sl]
        win = x
        for j in range(1, w):
            win = win + st_ref[ctx - j, :, sl]
        mix = win / float(w) - x
        y = jnp.dot(mix.astype(BF16), w_ref[g], preferred_element_type=F32)
        o_ref[:, sl] = (y * sc_ref[:, sl]).astype(o_ref.dtype)


def _pool_sample(u, state, w_pool, scale):
    bsz, width = u.shape
    return pl.pallas_call(
        functools.partial(_pool_sample_kernel, windows=POOL_WINDOWS),
        out_shape=jax.ShapeDtypeStruct((bsz, width), BF16),
        name="pool_sample",
        compiler_params=pltpu.CompilerParams(vmem_limit_bytes=V7X_VMEM_LIMIT_BYTES),
    )(u, state, w_pool, scale)


def kernel(x_prompt, x_sample, cache_k, cache_v, cache_mem_k, cache_mem_v, state_pool, page_table, mem_prompt, w_in, w_mem_kv, w_pool, pool_scale, w_br_attn, w_br_pool, w_br_mem, w_o, ln1_g, ln1_b, w_up, w_down, ln2_g, ln2_b):
    depth = w_in.shape[0]
    assert depth == 1, "single-layer trunk"
    batch, seq, d_model = x_prompt.shape
    dec_batch, dec_seq, _ = x_sample.shape
    assert dec_seq == 1
    _, n_phys, page, a_heads, a_hd = cache_k.shape
    assert page == PAGE_ROWS
    _, _, n_mem, m_heads, m_hd = cache_mem_k.shape
    pool_ctx, pool_width = state_pool.shape[2], state_pool.shape[3]
    a_width = a_heads * a_hd
    m_width = m_heads * m_hd
    alpha = float((2 * depth) ** 0.25)
    assert a_width == pool_width == m_width == COL_TILE
    k_tile, v_tile = 1, 2
    n_act_tiles = 5
    gate_cols = w_in.shape[2] - n_act_tiles * COL_TILE
    assert gate_cols == N_BRANCH * d_model

    w_pool_b = w_pool[0].astype(BF16)

    xs = x_sample.reshape(dec_batch, d_model)
    p3s = _project(xs, w_in[0], lambda j: j, n_act_tiles, tm=dec_batch, name="in_proj_sample")[0]
    q_s, k_s, v_s, u_s, mq_s = p3s[0], p3s[k_tile], p3s[v_tile], p3s[3], p3s[4]

    mp = batch * seq
    xp = x_prompt.reshape(mp, d_model)
    u_col, mq_col = 1, 2
    k3, k_heads, xb = _project(xp, w_in[0], lambda j: k_tile, 1, tm=1024, name="k_proj_prompt",
                               heads=a_heads, emit_x=True)
    v3, v_heads = _project(xb, w_in[0], lambda j: v_tile, 1, tm=1024, name="v_proj_prompt", heads=a_heads)
    p3 = _project(xb, w_in[0], lambda j: jnp.where(j == 0, 0, j + 2), n_act_tiles - 2,
                  tm=1024, name="in_proj_prompt")[0]
    attn_p, (w_up_b, w_down_b, w_o_b, w_ba_b, w_bp_b, w_bm_b, w_gate_b) = _moba_prompt(
        p3, k3[0], v3[0], batch=batch, seq=seq, heads=a_heads, hd=a_hd, q_col=0,
        cast=(w_up[0], w_down[0], w_o[0], w_br_attn[0], w_br_pool[0], w_br_mem[0],
              (w_in[0], n_act_tiles * COL_TILE, gate_cols)))
    pool_p = _pool_prompt(p3, w_pool_b, pool_scale, batch=batch, seq=seq, u_col=u_col)
    kv3 = _project(mem_prompt.reshape(batch * n_mem, d_model), w_mem_kv[0], lambda j: j, 2,
                   tm=batch * n_mem, name="mem_kv_prompt")[0]
    memo_p = _mem_prompt(p3, kv3, batch=batch, seq=seq, n_mem=n_mem, heads=m_heads, hd=m_hd, q_col=mq_col)
    mix_p = _merge(xb, attn_p, pool_p, memo_p, w_ba_b, w_bp_b, w_bm_b, w_gate_b, tm=512)
    h_p = _oproj(xp, mix_p, w_o_b, ln1_g, ln1_b, alpha=alpha, tm=512)
    y_p = _ffn(h_p, w_up_b, w_down_b, ln2_g, ln2_b, alpha=alpha, tm=1024, tk=512)

    memo_s = _mem_sample(mq_s.reshape(dec_batch, m_heads, m_hd), cache_mem_k, cache_mem_v)
    pool_s = _pool_sample(u_s, jnp.transpose(state_pool[0], (1, 0, 2)), w_pool_b, pool_scale)
    attn_s = _moba_sample(cache_k, cache_v, page_table, q_s, k_s, v_s)
    mix_s = _merge(xs, attn_s, pool_s, memo_s.reshape(dec_batch, m_width), w_ba_b, w_bp_b, w_bm_b, w_gate_b,
                   tm=dec_batch)
    h_s = _oproj(xs, mix_s, w_o_b, ln1_g, ln1_b, alpha=alpha, tm=dec_batch)
    y_s = _ffn(h_s, w_up_b, w_down_b, ln2_g, ln2_b, alpha=alpha, tm=dec_batch, tk=2048)

    kv_shape = (depth, batch, seq, a_heads, a_hd)
    mem_shape = (depth, batch, n_mem, m_heads, m_hd)
    new_pool_p = p3.reshape(-1, batch, seq, pool_width)[u_col, :, seq - pool_ctx:, :][None]
    new_pool_s = jnp.concatenate([state_pool[0][:, 1:, :], u_s[:, None, :]], axis=1)[None]
    skv_shape = (depth, dec_batch, dec_seq, a_heads, a_hd)
    return (y_p.reshape(batch, seq, d_model), y_s.reshape(dec_batch, dec_seq, d_model),
            k_heads.reshape(kv_shape), v_heads.reshape(kv_shape),
            kv3[0].reshape(mem_shape), kv3[1].reshape(mem_shape),
            new_pool_p,
            k_s.reshape(skv_shape), v_s.reshape(skv_shape),
            new_pool_s)
```

```python
import functools

import jax
import jax.numpy as jnp
from jax import lax
from jax.experimental import pallas as pl
from jax.experimental.pallas import tpu as pltpu
from jax.experimental.pallas import tpu_sc as plsc

MOBA_BLOCK = 256
MOBA_TOPK = 3
POOL_WINDOWS = (2, 4, 8, 16)
N_BRANCH = 3
LN_EPS = 1e-5
NEG = -1e30
PAGE_ROWS = 128

V7X_VMEM_LIMIT_BYTES = 60000 * 1024
V7X_SC_LANES = 16
COL_TILE = 1024
FFN_SLAB = 512
CAST_PARTS = 4
BF16_ROWS = 16
MEAN_ROWS = 8
SC_CHUNK_ROWS = 256

BF16 = jnp.bfloat16
F32 = jnp.float32
NT = (((1,), (1,)), ((), ()))


def _params(*semantics):
    return pltpu.CompilerParams(dimension_semantics=semantics,
                                vmem_limit_bytes=V7X_VMEM_LIMIT_BYTES)


def _layer_norm(x, g, b):
    mu = jnp.mean(x, axis=-1, keepdims=True)
    xc = x - mu
    var = jnp.mean(xc * xc, axis=-1, keepdims=True)
    return xc * lax.rsqrt(var + LN_EPS) * g + b


def _proj_kernel(x_ref, w_ref, *rest, heads, emit_means, emit_x):
    wb_ref = rest[-1]

    @pl.when(pl.program_id(1) == 0)
    def _():
        wb_ref[...] = w_ref[...].astype(BF16)

    xb = x_ref[...].astype(BF16)
    y = jnp.dot(xb, wb_ref[...], preferred_element_type=F32)
    if not heads:
        rest[0][0] = y
        return
    oh_ref, ob_ref = rest[0], rest[1]
    tm, hd = y.shape[0], y.shape[1] // heads
    for h in range(heads):
        oh_ref[pl.ds(h, tm, stride=heads), :] = y[:, h * hd:(h + 1) * hd]
    ob_ref[...] = y.astype(BF16)
    if emit_means:
        mean_ref = rest[2]
        n_blk = tm // MOBA_BLOCK
        means = [jnp.mean(y[n * MOBA_BLOCK:(n + 1) * MOBA_BLOCK], axis=0, keepdims=True) for n in range(n_blk)]
        means.append(jnp.zeros((mean_ref.shape[1] - n_blk, y.shape[1]), F32))
        mean_ref[0] = jnp.concatenate(means, axis=0)
    if emit_x:
        rest[-2][...] = xb


def _project(x, w, col_of, n_cols, *, tm, name, heads=0, emit_means=False, emit_x=False):
    m, k = x.shape
    tn = COL_TILE
    if heads:
        assert n_cols == 1
        out_shape = [jax.ShapeDtypeStruct((m * heads, tn // heads), F32), jax.ShapeDtypeStruct((m, tn), BF16)]
        out_specs = [pl.BlockSpec((tm * heads, tn // heads), lambda j, i: (i, 0)),
                     pl.BlockSpec((tm, tn), lambda j, i: (i, 0))]
    else:
        out_shape = [jax.ShapeDtypeStruct((n_cols, m, tn), F32)]
        out_specs = [pl.BlockSpec((1, tm, tn), lambda j, i: (j, i, 0))]
    if emit_means:
        assert heads and tm % MOBA_BLOCK == 0 and tm // MOBA_BLOCK <= MEAN_ROWS
        out_shape.append(jax.ShapeDtypeStruct((m // tm, MEAN_ROWS, tn), F32))
        out_specs.append(pl.BlockSpec((1, MEAN_ROWS, tn), lambda j, i: (i, 0, 0)))
    if emit_x:
        assert heads
        out_shape.append(jax.ShapeDtypeStruct((m, k), BF16))
        out_specs.append(pl.BlockSpec((tm, k), lambda j, i: (i, 0)))
    return pl.pallas_call(
        functools.partial(_proj_kernel, heads=heads, emit_means=emit_means, emit_x=emit_x),
        out_shape=out_shape,
        grid=(n_cols, m // tm),
        in_specs=[pl.BlockSpec((tm, k), lambda j, i: (i, 0)),
                  pl.BlockSpec((k, tn), lambda j, i: (0, col_of(j)))],
        out_specs=out_specs,
        scratch_shapes=[pltpu.VMEM((k, tn), BF16)],
        compiler_params=_params("arbitrary", "arbitrary"),
        name=name,
    )(x, w)


def _moba_prompt_kernel(q_ref, k_ref, v_ref, mean_ref, *rest, seq, blk, topk, cast_blocks):
    n_src = sum(cast_blocks)
    cast_src, o_ref, cast_dst = rest[:n_src], rest[n_src], rest[n_src + 1:]
    nb = seq // blk
    hd = q_ref.shape[-1]
    q = q_ref[0]
    per_tile = nb // mean_ref.shape[0]
    means = jnp.concatenate([mean_ref[t, :per_tile, :] for t in range(mean_ref.shape[0])], axis=0)
    qs = q * (hd ** -0.5)
    kb = k_ref[...]
    vb = v_ref[...]
    key_blk = lax.broadcasted_iota(jnp.int32, (seq, hd), 0) // blk
    lane = lax.broadcasted_iota(jnp.int32, (seq, hd), 1)
    k_ext = jnp.concatenate([kb, jnp.where(key_blk == lane, 1.0, 0.0).astype(BF16)], axis=1)
    row = lax.broadcasted_iota(jnp.int32, (blk, blk), 0)
    col = lax.broadcasted_iota(jnp.int32, (blk, blk), 1)
    causal = col <= row
    blk_id = lax.broadcasted_iota(jnp.int32, (nb, blk), 0)
    for i in range(nb):
        qi = qs[i * blk:(i + 1) * blk]
        if i > topk:
            gate = lax.dot_general(means, q[i * blk:(i + 1) * blk], NT,
                                   precision=lax.Precision.HIGHEST, preferred_element_type=F32)
            beaten = jnp.zeros((nb, blk), jnp.int32)
            for m in range(i):
                gm = gate[m:m + 1, :]
                beaten = beaten + jnp.where(m < blk_id, jnp.where(gm >= gate, 1, 0), jnp.where(gm > gate, 1, 0))
            bias = jnp.where((beaten < topk) | (blk_id >= i), 0.0, NEG)
            bias = jnp.concatenate([bias, jnp.zeros((hd - nb, blk), F32)], axis=0).T
            q_ext = jnp.concatenate([qi.astype(BF16), bias.astype(BF16)], axis=1)
            s = lax.dot_general(q_ext, k_ext[:(i + 1) * blk], NT, preferred_element_type=F32)
        else:
            s = lax.dot_general(qi.astype(BF16), kb[:(i + 1) * blk], NT, preferred_element_type=F32)
        own = jnp.where(causal, s[:, i * blk:(i + 1) * blk], NEG)
        s = jnp.concatenate([s[:, :i * blk], own], axis=1) if i else own
        mx = jnp.max(s, axis=-1, keepdims=True)
        p = jnp.exp(s - mx)
        den = jnp.sum(p, axis=-1, keepdims=True)
        o = jnp.dot(p.astype(BF16), vb[:(i + 1) * blk], preferred_element_type=F32)
        o_ref[i * blk:(i + 1) * blk, :] = (o / den).astype(o_ref.dtype)
        first = 0
        for dst, n_blocks in zip(cast_dst, cast_blocks):
            parts = min(CAST_PARTS, dst.shape[0] // BF16_ROWS)
            every = nb // parts
            if i % every == every - 1:
                rows = dst.shape[0] // parts
                rsl = slice((i // every) * rows, (i // every + 1) * rows)
                for c, src in enumerate(cast_src[first:first + n_blocks]):
                    w = src.shape[1]
                    dst[rsl, c * w:(c + 1) * w] = src[rsl, :].astype(BF16)
            first += n_blocks


def _moba_prompt(p3, k2, v2, k_means, *, batch, seq, heads, hd, q_col, cast=()):
    assert heads * hd == COL_TILE
    steps = batch * heads
    tiles_per_row = k_means.shape[0] // batch
    assert k_means.shape[0] == batch * tiles_per_row and (seq // MOBA_BLOCK) % tiles_per_row == 0
    kv = pl.BlockSpec((seq, hd), lambda b, h: (b, h))
    srcs, src_specs, dst_shapes, dst_specs, cast_blocks = [], [], [], [], []
    for entry in cast:
        w, col0, cols = entry if isinstance(entry, tuple) else (entry, 0, entry.shape[1])
        rows = w.shape[0] // steps
        assert w.shape[0] % (steps * BF16_ROWS) == 0
        if col0 == 0 and cols == w.shape[1]:
            blocks = [(cols, 0)]
        else:
            assert col0 % COL_TILE == 0 and cols % COL_TILE == 0
            blocks = [(COL_TILE, col0 // COL_TILE + c) for c in range(cols // COL_TILE)]
        for width, cb in blocks:
            srcs.append(w)
            src_specs.append(pl.BlockSpec((rows, width), lambda b, h, cb=cb: (b * heads + h, cb)))
        cast_blocks.append(len(blocks))
        dst_shapes.append(jax.ShapeDtypeStruct((w.shape[0], cols), BF16))
        dst_specs.append(pl.BlockSpec((rows, cols), lambda b, h: (b * heads + h, 0)))
    kern = functools.partial(_moba_prompt_kernel, seq=seq, blk=MOBA_BLOCK, topk=MOBA_TOPK,
                             cast_blocks=tuple(cast_blocks))
    out = pl.pallas_call(
        kern,
        out_shape=[jax.ShapeDtypeStruct((batch * seq, heads * hd), BF16)] + dst_shapes,
        grid=(batch, heads),
        in_specs=[pl.BlockSpec((1, seq, hd), lambda b, h: (q_col, b, h)), kv, kv,
                  pl.BlockSpec((tiles_per_row, MEAN_ROWS, hd), lambda b, h: (b, 0, h))] + src_specs,
        out_specs=[kv] + dst_specs,
        compiler_params=_params("arbitrary", "arbitrary"),
        name="moba_prompt",
    )(p3, k2, v2, k_means, *srcs)
    return out[0], out[1:]


def _pool_prompt_kernel(u_ref, w_ref, sc_ref, o_ref, *, windows):
    seq = u_ref.shape[1]
    gc = w_ref.shape[-1]
    t = lax.broadcasted_iota(jnp.int32, (seq, 1), 0)
    for g, w in enumerate(windows):
        x = u_ref[0, :, g * gc:(g + 1) * gc]
        win = x
        width = 1
        while width < w:
            shifted = pltpu.roll(win, width, 0)
            win = win + jnp.where(t >= width, shifted, 0.0)
            width *= 2
        cnt = jnp.minimum(w, t + 1).astype(F32)
        mix = win / cnt - x
        y = jnp.dot(mix.astype(BF16), w_ref[g], preferred_element_type=F32)
        o_ref[:, g * gc:(g + 1) * gc] = (y * sc_ref[:, g * gc:(g + 1) * gc]).astype(o_ref.dtype)


def _pool_prompt(p3, w_pool, scale, *, batch, seq, u_col):
    groups, gc, _ = w_pool.shape
    width = groups * gc
    assert width == COL_TILE
    kern = functools.partial(_pool_prompt_kernel, windows=POOL_WINDOWS)
    return pl.pallas_call(
        kern,
        out_shape=jax.ShapeDtypeStruct((batch * seq, width), BF16),
        grid=(batch,),
        in_specs=[pl.BlockSpec((1, seq, width), lambda b: (u_col, b, 0)),
                  pl.BlockSpec((groups, gc, gc), lambda b: (0, 0, 0)),
                  pl.BlockSpec((1, width), lambda b: (0, 0))],
        out_specs=pl.BlockSpec((seq, width), lambda b: (b, 0)),
        compiler_params=_params("arbitrary"),
        name="pool_prompt",
    )(p3, w_pool, scale)


def _mem_prompt_kernel(q_ref, k_ref, v_ref, o_ref):
    hd = q_ref.shape[-1]
    qb = (q_ref[0] * (hd ** -0.5)).astype(BF16)
    s = lax.dot_general(qb, k_ref[0].astype(BF16), NT, preferred_element_type=F32)
    mx = jnp.max(s, axis=-1, keepdims=True)
    p = jnp.exp(s - mx)
    den = jnp.sum(p, axis=-1, keepdims=True)
    o = jnp.dot(p.astype(BF16), v_ref[0].astype(BF16), preferred_element_type=F32)
    o_ref[...] = (o / den).astype(o_ref.dtype)


def _mem_prompt(p3, kv3, *, batch, seq, n_mem, heads, hd, q_col):
    assert heads * hd == COL_TILE
    return pl.pallas_call(
        _mem_prompt_kernel,
        out_shape=jax.ShapeDtypeStruct((batch * seq, heads * hd), BF16),
        grid=(batch, heads),
        in_specs=[pl.BlockSpec((1, seq, hd), lambda b, h: (q_col, b, h)),
                  pl.BlockSpec((1, n_mem, hd), lambda b, h: (0, b, h)),
                  pl.BlockSpec((1, n_mem, hd), lambda b, h: (1, b, h))],
        out_specs=pl.BlockSpec((seq, hd), lambda b, h: (b, h)),
        compiler_params=_params("arbitrary", "arbitrary"),
        name="mem_prompt",
    )(p3, kv3, kv3)


def _merge_kernel(x_ref, a_ref, p_ref, m_ref, wa_ref, wp_ref, wm_ref, ga_ref, gp_ref, gm_ref, o_ref):
    xb = x_ref[...].astype(BF16)

    def branch(b_ref, w_ref, g_ref):
        gate = jnp.dot(xb, g_ref[...], preferred_element_type=F32)
        y = jnp.dot(b_ref[...].astype(BF16), w_ref[...], preferred_element_type=F32)
        return jax.nn.sigmoid(gate) * y
    mix = branch(a_ref, wa_ref, ga_ref) + branch(p_ref, wp_ref, gp_ref) + branch(m_ref, wm_ref, gm_ref)
    o_ref[...] = mix.astype(o_ref.dtype)


def _merge(x, attn, pool, memo, w_ba, w_bp, w_bm, w_gate, *, tm):
    m, kw = attn.shape
    d = w_ba.shape[1]
    tn = COL_TILE
    per_gate = d // tn
    row = lambda w: pl.BlockSpec((tm, w), lambda j, i: (i, 0))
    wsp = pl.BlockSpec((kw, tn), lambda j, i: (0, j))
    gsp = lambda c: pl.BlockSpec((d, tn), lambda j, i: (0, c * per_gate + j))
    return pl.pallas_call(
        _merge_kernel,
        out_shape=jax.ShapeDtypeStruct((m, d), BF16),
        grid=(d // tn, m // tm),
        in_specs=[row(d), row(kw), row(kw), row(kw), wsp, wsp, wsp, gsp(0), gsp(1), gsp(2)],
        out_specs=pl.BlockSpec((tm, tn), lambda j, i: (i, j)),
        compiler_params=_params("arbitrary", "arbitrary"),
        name="merge",
    )(x, attn, pool, memo, w_ba, w_bp, w_bm, w_gate, w_gate, w_gate)


def _oproj_kernel(x_ref, mix_ref, w_ref, g_ref, b_ref, o_ref, *, alpha):
    y = jnp.dot(mix_ref[...], w_ref[...], preferred_element_type=F32)
    o_ref[...] = _layer_norm(alpha * x_ref[...] + y, g_ref[...], b_ref[...])


def _oproj(x, mix, w_o, g, b, *, alpha, tm):
    m, d = x.shape
    row = pl.BlockSpec((tm, d), lambda i: (i, 0))
    vec = pl.BlockSpec((1, d), lambda i: (0, 0))
    return pl.pallas_call(
        functools.partial(_oproj_kernel, alpha=alpha),
        out_shape=jax.ShapeDtypeStruct((m, d), F32),
        grid=(m // tm,),
        in_specs=[row, row, pl.BlockSpec((d, d), lambda i: (0, 0)), vec, vec],
        out_specs=row,
        compiler_params=_params("arbitrary"),
        name="oproj_ln1",
    )(x, mix, w_o, g, b)


def _ffn_kernel(h_ref, wu_ref, wd_ref, g_ref, b_ref, o_ref, hb_ref, *, alpha):
    kk = pl.program_id(1)

    @pl.when(kk == 0)
    def _():
        hb_ref[...] = h_ref[...].astype(BF16)
        o_ref[...] = jnp.zeros_like(o_ref)

    a = jnp.dot(hb_ref[...], wu_ref[...], preferred_element_type=F32)
    a = jnp.square(jnp.maximum(a, 0.0)).astype(BF16)
    d = o_ref.shape[1]
    for c in range(0, d, FFN_SLAB):
        o_ref[:, c:c + FFN_SLAB] += jnp.dot(a, wd_ref[:, c:c + FFN_SLAB], preferred_element_type=F32)

    @pl.when(kk == pl.num_programs(1) - 1)
    def _():
        o_ref[...] = _layer_norm(alpha * h_ref[...] + o_ref[...], g_ref[...], b_ref[...])


def _ffn(h, w_up, w_down, g, b, *, alpha, tm, tk):
    m, d = h.shape
    dff = w_up.shape[1]
    row = pl.BlockSpec((tm, d), lambda i, k: (i, 0))
    vec = pl.BlockSpec((1, d), lambda i, k: (0, 0))
    return pl.pallas_call(
        functools.partial(_ffn_kernel, alpha=alpha),
        out_shape=jax.ShapeDtypeStruct((m, d), F32),
        grid=(m // tm, dff // tk),
        in_specs=[row,
                  pl.BlockSpec((d, tk), lambda i, k: (0, k)),
                  pl.BlockSpec((tk, d), lambda i, k: (k, 0)),
                  vec, vec],
        out_specs=row,
        scratch_shapes=[pltpu.VMEM((tm, d), BF16)],
        compiler_params=_params("arbitrary", "arbitrary"),
        name="ffn_ln2",
    )(h, w_up, w_down, g, b)


def _moba_sample(cache_k, cache_v, page_table, q, k_new, v_new):
    _, n_phys, page, heads, hd = cache_k.shape
    bsz, n_pages = page_table.shape
    L = V7X_SC_LANES
    topk = MOBA_TOPK
    per_blk = MOBA_BLOCK // page
    n_blk = n_pages // per_blk
    rows_per_page = page * heads
    n_ch = rows_per_page // SC_CHUNK_ROWS
    items_per_blk = per_blk * n_ch
    n_items = n_pages * n_ch
    width = heads * hd
    hv = hd // L
    groups = page // L
    n_sel_pages = topk * per_blk
    gs = n_blk + 1
    scale = hd ** -0.5
    assert hd & (hd - 1) == 0 and n_pages % L == 0 and SC_CHUNK_ROWS % heads == 0
    krows = cache_k.reshape(n_phys * rows_per_page, hd)
    vrows = cache_v.reshape(n_phys * rows_per_page, hd)
    mesh = plsc.VectorSubcoreMesh(core_axis_name="core", subcore_axis_name="subcore")
    assert bsz == mesh.num_cores * mesh.num_subcores, "one batch row per vector subcore"

    def kern(kr_hbm, vr_hbm, pt_hbm, q_hbm, kn_hbm, vn_hbm, o_hbm,
             buf, acc, q_v, kn_v, vn_v, pt_v, rowbuf, idx_v, s_v, out_v, gate_s, sel_s, sems, rsem):
        b = lax.axis_index("core") * mesh.num_subcores + lax.axis_index("subcore")
        pltpu.sync_copy(q_hbm.at[b], q_v)
        pltpu.sync_copy(kn_hbm.at[b], kn_v)
        pltpu.sync_copy(vn_hbm.at[b], vn_v)
        pltpu.sync_copy(pt_hbm.at[b], pt_v)
        lane = lax.iota(jnp.int32, L)

        def phys_page(pg):
            group = pt_v[pl.ds((pg // L) * L, L)]
            return jnp.sum(jnp.where(lane == pg % L, group, 0))

        def chunk_start(item, slot):
            start = phys_page(item // n_ch) * rows_per_page + (item % n_ch) * SC_CHUNK_ROWS
            pltpu.make_async_copy(kr_hbm.at[pl.ds(pl.multiple_of(start, SC_CHUNK_ROWS), SC_CHUNK_ROWS)],
                                  buf.at[slot], sems.at[slot]).start()

        def chunk_wait(slot):
            pltpu.make_async_copy(kr_hbm.at[pl.ds(0, SC_CHUNK_ROWS)], buf.at[slot], sems.at[slot]).wait()

        chunk_start(0, 0)

        @pl.loop(0, n_blk)
        def _(n):
            for h in range(heads):
                for c in range(hv):
                    acc[h, pl.ds(c * L, L)] = jnp.zeros((L,), F32)
            for j in range(items_per_blk):
                slot = j % 2
                item = n * items_per_blk + j
                chunk_wait(slot)

                @pl.when(item + 1 < n_items)
                def _():
                    chunk_start(item + 1, 1 - slot)

                @pl.loop(0, heads * hv)
                def _(hc):
                    h = hc // hv
                    cols = pl.ds((hc % hv) * L, L)
                    rows = [t * heads + h for t in range(SC_CHUNK_ROWS // heads)]
                    parts = [buf[slot, r, cols] for r in rows[:4]]
                    for i, r in enumerate(rows[4:]):
                        parts[i % 4] = parts[i % 4] + buf[slot, r, cols]
                    plsc.addupdate(acc.at[h, cols], (parts[0] + parts[1]) + (parts[2] + parts[3]))
            for h in range(heads):
                prod = acc[h, pl.ds(0, L)] * q_v[pl.ds(h * hd, L)]
                for c in range(1, hv):
                    prod = prod + acc[h, pl.ds(c * L, L)] * q_v[pl.ds(h * hd + c * L, L)]
                gate_s[h * gs + n] = jnp.sum(prod / MOBA_BLOCK)

        own_blk = n_blk
        for h in range(heads):
            open_prod = kn_v[pl.ds(h * hd, L)] * q_v[pl.ds(h * hd, L)]
            for c in range(1, hv):
                open_prod = open_prod + kn_v[pl.ds(h * hd + c * L, L)] * q_v[pl.ds(h * hd + c * L, L)]
            gate_s[h * gs + n_blk] = jnp.where(n_blk < own_blk, jnp.sum(open_prod / MOBA_BLOCK), jnp.float32(NEG))
            for j in range(topk):
                def scan(n, carry, h=h):
                    best, bi = carry
                    g = gate_s[h * gs + n]
                    better = g > best
                    return jnp.where(better, g, best), jnp.where(better, n, bi)
                _, bi = lax.fori_loop(1, gs, scan, (gate_s[h * gs], jnp.int32(0)))
                gate_s[h * gs + bi] = jnp.float32(-jnp.inf)
                sel_s[h * topk + j] = bi

        def sel_page(h, jr):
            return sel_s[h * topk + jr // per_blk] * per_blk + jr % per_blk

        def rows_copy(src_hbm, slot):
            return pltpu.make_async_copy(src_hbm.at[idx_v.at[slot]], rowbuf.at[slot], rsem.at[slot])

        def rows_start(src_hbm, h, pg, slot):
            first = phys_page(pg) * page
            for c in range(groups):
                idx_v[slot, pl.ds(c * L, L)] = (first + c * L + lane) * heads + h
            rows_copy(src_hbm, slot).start()

        @pl.loop(0, heads)
        def _(h):
            rows_start(kr_hbm, h, sel_page(h, 0), 0)
            for jr in range(n_sel_pages):
                slot = jr % 2
                rows_copy(kr_hbm, slot).wait()
                if jr + 1 < n_sel_pages:
                    rows_start(kr_hbm, h, sel_page(h, jr + 1), 1 - slot)
                else:
                    rows_start(vr_hbm, h, sel_page(h, 0), 1 - slot)

                def dots(d, accs, slot=slot):
                    col = (d + lane) & (hd - 1)
                    qd = plsc.load_gather(q_v, [h * hd + col])
                    return tuple(accs[g] + plsc.load_gather(rowbuf.at[slot], [g * L + lane, col]) * qd
                                 for g in range(groups))
                accs = lax.fori_loop(0, hd, dots, tuple(jnp.zeros((L,), F32) for _ in range(groups)))
                for g in range(groups):
                    s_v[pl.ds(jr * page + g * L, L)] = accs[g] * scale
            own = q_v[pl.ds(h * hd, L)] * kn_v[pl.ds(h * hd, L)]
            for c in range(1, hv):
                own = own + q_v[pl.ds(h * hd + c * L, L)] * kn_v[pl.ds(h * hd + c * L, L)]
            s_own = jnp.sum(own) * scale
            n_vec = n_sel_pages * groups

            def vmax(i, m):
                return jnp.maximum(m, s_v[pl.ds(i * L, L)])
            mx = jnp.maximum(jnp.max(lax.fori_loop(0, n_vec, vmax, jnp.full((L,), NEG, F32))), s_own)

            def expsum(i, tot):
                p = jnp.exp(s_v[pl.ds(i * L, L)] - mx)
                s_v[pl.ds(i * L, L)] = p
                return tot + p
            p_own = jnp.max(jnp.exp(jnp.full((L,), s_own - mx, F32)))
            den = jnp.sum(lax.fori_loop(0, n_vec, expsum, jnp.zeros((L,), F32))) + p_own
            outs = tuple(p_own * vn_v[pl.ds(h * hd + c * L, L)] for c in range(hv))
            v_first = n_sel_pages % 2
            for jr in range(n_sel_pages):
                slot = (v_first + jr) % 2
                rows_copy(vr_hbm, slot).wait()
                if jr + 1 < n_sel_pages:
                    rows_start(vr_hbm, h, sel_page(h, jr + 1), 1 - slot)

                def pv(t, o, slot=slot, jr=jr):
                    p = plsc.load_gather(s_v, [jnp.full((L,), jr * page, jnp.int32) + t])
                    return tuple(o[c] + p * rowbuf[slot, t, pl.ds(c * L, L)] for c in range(hv))
                outs = lax.fori_loop(0, page, pv, outs)
            for c in range(hv):
                out_v[pl.ds(h * hd + c * L, L)] = outs[c] / den
        pltpu.sync_copy(out_v, o_hbm.at[b])

    return pl.kernel(
        kern,
        out_type=jax.ShapeDtypeStruct((bsz, width), F32),
        mesh=mesh,
        scratch_types=[pltpu.VMEM((2, SC_CHUNK_ROWS, hd), F32),
                       pltpu.VMEM((heads, hd), F32),
                       pltpu.VMEM((width,), F32),
                       pltpu.VMEM((width,), F32),
                       pltpu.VMEM((width,), F32),
                       pltpu.VMEM((n_pages,), jnp.int32),
                       pltpu.VMEM((2, page, hd), F32),
                       pltpu.VMEM((2, page), jnp.int32),
                       pltpu.VMEM((n_sel_pages * page,), F32),
                       pltpu.VMEM((width,), F32),
                       pltpu.SMEM((heads * gs,), F32),
                       pltpu.SMEM((heads * topk,), jnp.int32),
                       pltpu.SemaphoreType.DMA((2,)),
                       pltpu.SemaphoreType.DMA((2,))],
        compiler_params=pltpu.CompilerParams(needs_layout_passes=False),
        name="moba_sample",
    )(krows, vrows, page_table, q, k_new, v_new)


def _mem_sample_kernel(q_ref, k_ref, v_ref, o_ref):
    hd = q_ref.shape[-1]
    q = q_ref[0] * (hd ** -0.5)
    s = jnp.sum(k_ref[0, 0] * q, axis=-1, keepdims=True)
    mx = jnp.max(s, axis=0, keepdims=True)
    p = jnp.exp(s - mx)
    den = jnp.sum(p, axis=0)
    o = jnp.sum(p * v_ref[0, 0], axis=0)
    o_ref[0] = o / den


def _mem_sample(q, mem_k, mem_v):
    _, bsz, n_mem, heads, hd = mem_k.shape
    tok = pl.BlockSpec((1, heads, hd), lambda b: (b, 0, 0))
    mem = pl.BlockSpec((1, 1, n_mem, heads, hd), lambda b: (0, b, 0, 0, 0))
    return pl.pallas_call(
        _mem_sample_kernel,
        out_shape=jax.ShapeDtypeStruct((bsz, heads, hd), F32),
        grid=(bsz,),
        in_specs=[tok, mem, mem],
        out_specs=tok,
        compiler_params=_params("arbitrary"),
        name="mem_sample",
    )(q, mem_k, mem_v)


def _pool_sample_kernel(u_ref, st_ref, w_ref, sc_ref, o_ref, *, windows):
    ctx = st_ref.shape[0]
    gc = w_ref.shape[-1]
    for g, w in enumerate(windows):
        sl = slice(g * gc, (g + 1) * gc)
        x = u_ref[:, sl]
        win = x
        for j in range(1, w):
            win = win + st_ref[ctx - j, :, sl]
        mix = win / float(w) - x
        y = jnp.dot(mix.astype(BF16), w_ref[g], preferred_element_type=F32)
        o_ref[:, sl] = (y * sc_ref[:, sl]).astype(o_ref.dtype)


def _pool_sample(u, state, w_pool, scale):
    bsz, width = u.shape
    return pl.pallas_call(
        functools.partial(_pool_sample_kernel, windows=POOL_WINDOWS),
        out_shape=jax.ShapeDtypeStruct((bsz, width), BF16),
        name="pool_sample",
        compiler_params=pltpu.CompilerParams(vmem_limit_bytes=V7X_VMEM_LIMIT_BYTES),
    )(u, state, w_pool, scale)


def kernel(x_prompt, x_sample, cache_k, cache_v, cache_mem_k, cache_mem_v, state_pool, page_table, mem_prompt, w_in, w_mem_kv, w_pool, pool_scale, w_br_attn, w_br_pool, w_br_mem, w_o, ln1_g, ln1_b, w_up, w_down, ln2_g, ln2_b):
    depth = w_in.shape[0]
    assert depth == 1, "single-layer trunk"
    batch, seq, d_model = x_prompt.shape
    dec_batch, dec_seq, _ = x_sample.shape
    assert dec_seq == 1
    _, n_phys, page, a_heads, a_hd = cache_k.shape
    assert page == PAGE_ROWS
    _, _, n_mem, m_heads, m_hd = cache_mem_k.shape
    pool_ctx, pool_width = state_pool.shape[2], state_pool.shape[3]
    a_width = a_heads * a_hd
    m_width = m_heads * m_hd
    alpha = float((2 * depth) ** 0.25)
    assert a_width == pool_width == m_width == COL_TILE
    k_tile, v_tile = 1, 2
    n_act_tiles = 5
    gate_cols = w_in.shape[2] - n_act_tiles * COL_TILE
    assert gate_cols == N_BRANCH * d_model

    w_pool_b = w_pool[0].astype(BF16)

    xs = x_sample.reshape(dec_batch, d_model)
    p3s = _project(xs, w_in[0], lambda j: j, n_act_tiles, tm=dec_batch, name="in_proj_sample")[0]
    q_s, k_s, v_s, u_s, mq_s = p3s[0], p3s[k_tile], p3s[v_tile], p3s[3], p3s[4]

    mp = batch * seq
    xp = x_prompt.reshape(mp, d_model)
    u_col, mq_col = 1, 2
    k_heads, k_b, k_means, xb = _project(xp, w_in[0], lambda j: k_tile, 1, tm=1024, name="k_proj_prompt",
                                         heads=a_heads, emit_means=True, emit_x=True)
    v_heads, v_b = _project(xb, w_in[0], lambda j: v_tile, 1, tm=1024, name="v_proj_prompt", heads=a_heads)
    p3 = _project(xb, w_in[0], lambda j: jnp.where(j == 0, 0, j + 2), n_act_tiles - 2,
                  tm=1024, name="in_proj_prompt")[0]
    attn_p, (w_up_b, w_down_b, w_o_b, w_ba_b, w_bp_b, w_bm_b, w_gate_b) = _moba_prompt(
        p3, k_b, v_b, k_means, batch=batch, seq=seq, heads=a_heads, hd=a_hd, q_col=0,
        cast=(w_up[0], w_down[0], w_o[0], w_br_attn[0], w_br_pool[0], w_br_mem[0],
              (w_in[0], n_act_tiles * COL_TILE, gate_cols)))
    pool_p = _pool_prompt(p3, w_pool_b, pool_scale, batch=batch, seq=seq, u_col=u_col)
    kv3 = _project(mem_prompt.reshape(batch * n_mem, d_model), w_mem_kv[0], lambda j: j, 2,
                   tm=n_mem, name="mem_kv_prompt")[0]
    memo_p = _mem_prompt(p3, kv3, batch=batch, seq=seq, n_mem=n_mem, heads=m_heads, hd=m_hd, q_col=mq_col)
    mix_p = _merge(xb, attn_p, pool_p, memo_p, w_ba_b, w_bp_b, w_bm_b, w_gate_b, tm=512)
    h_p = _oproj(xp, mix_p, w_o_b, ln1_g, ln1_b, alpha=alpha, tm=512)
    y_p = _ffn(h_p, w_up_b, w_down_b, ln2_g, ln2_b, alpha=alpha, tm=1024, tk=512)

    memo_s = _mem_sample(mq_s.reshape(dec_batch, m_heads, m_hd), cache_mem_k, cache_mem_v)
    pool_s = _pool_sample(u_s, jnp.transpose(state_pool[0], (1, 0, 2)), w_pool_b, pool_scale)
    attn_s = _moba_sample(cache_k, cache_v, page_table, q_s, k_s, v_s)
    mix_s = _merge(xs, attn_s, pool_s, memo_s.reshape(dec_batch, m_width), w_ba_b, w_bp_b, w_bm_b, w_gate_b,
                   tm=dec_batch)
    h_s = _oproj(xs, mix_s, w_o_b, ln1_g, ln1_b, alpha=alpha, tm=dec_batch)
    y_s = _ffn(h_s, w_up_b, w_down_b, ln2_g, ln2_b, alpha=alpha, tm=dec_batch, tk=2048)

    kv_shape = (depth, batch, seq, a_heads, a_hd)
    mem_shape = (depth, batch, n_mem, m_heads, m_hd)
    new_pool_p = p3.reshape(-1, batch, seq, pool_width)[u_col, :, seq - pool_ctx:, :][None]
    new_pool_s = jnp.concatenate([state_pool[0][:, 1:, :], u_s[:, None, :]], axis=1)[None]
    skv_shape = (depth, dec_batch, dec_seq, a_heads, a_hd)
    return (y_p.reshape(batch, seq, d_model), y_s.reshape(dec_batch, dec_seq, d_model),
            k_heads.reshape(kv_shape), v_heads.reshape(kv_shape),
            kv3[0].reshape(mem_shape), kv3[1].reshape(mem_shape),
            new_pool_p,
            k_s.reshape(skv_shape), v_s.reshape(skv_shape),
            new_pool_s)
```

```python
import functools

import jax
import jax.numpy as jnp
from jax import lax
from jax.experimental import pallas as pl
from jax.experimental.pallas import tpu as pltpu
from jax.experimental.pallas import tpu_sc as plsc

MOBA_BLOCK = 256
MOBA_TOPK = 3
POOL_WINDOWS = (2, 4, 8, 16)
N_BRANCH = 3
LN_EPS = 1e-5
NEG = -1e30
PAGE_ROWS = 128

V7X_VMEM_LIMIT_BYTES = 60000 * 1024
V7X_SC_LANES = 16
COL_TILE = 1024
FFN_SLAB = 512
CAST_PARTS = 4
BF16_ROWS = 16
MEAN_ROWS = 8
SC_CHUNK_ROWS = 256

BF16 = jnp.bfloat16
F32 = jnp.float32
NT = (((1,), (1,)), ((), ()))


def _params(*semantics):
    return pltpu.CompilerParams(dimension_semantics=semantics,
                                vmem_limit_bytes=V7X_VMEM_LIMIT_BYTES)


def _layer_norm(x, g, b):
    mu = jnp.mean(x, axis=-1, keepdims=True)
    xc = x - mu
    var = jnp.mean(xc * xc, axis=-1, keepdims=True)
    return xc * lax.rsqrt(var + LN_EPS) * g + b


def _proj_kernel(x_ref, w_ref, *rest, heads, emit_means, emit_x):
    wb_ref = rest[-1]

    @pl.when(pl.program_id(1) == 0)
    def _():
        wb_ref[...] = w_ref[...].astype(BF16)

    xb = x_ref[...].astype(BF16)
    y = jnp.dot(xb, wb_ref[...], preferred_element_type=F32)
    if not heads:
        rest[0][0] = y
        return
    oh_ref, ob_ref = rest[0], rest[1]
    tm, hd = y.shape[0], y.shape[1] // heads
    for h in range(heads):
        oh_ref[pl.ds(h, tm, stride=heads), :] = y[:, h * hd:(h + 1) * hd]
    ob_ref[...] = y.astype(BF16)
    if emit_means:
        mean_ref = rest[2]
        n_blk = tm // MOBA_BLOCK
        means = [jnp.mean(y[n * MOBA_BLOCK:(n + 1) * MOBA_BLOCK], axis=0, keepdims=True) for n in range(n_blk)]
        means.append(jnp.zeros((mean_ref.shape[1] - n_blk, y.shape[1]), F32))
        mean_ref[0] = jnp.concatenate(means, axis=0)
    if emit_x:
        rest[-2][...] = xb


def _project(x, w, col_of, n_cols, *, tm, name, heads=0, emit_means=False, emit_x=False):
    m, k = x.shape
    tn = COL_TILE
    if heads:
        assert n_cols == 1
        out_shape = [jax.ShapeDtypeStruct((m * heads, tn // heads), F32), jax.ShapeDtypeStruct((m, tn), BF16)]
        out_specs = [pl.BlockSpec((tm * heads, tn // heads), lambda j, i: (i, 0)),
                     pl.BlockSpec((tm, tn), lambda j, i: (i, 0))]
    else:
        out_shape = [jax.ShapeDtypeStruct((n_cols, m, tn), F32)]
        out_specs = [pl.BlockSpec((1, tm, tn), lambda j, i: (j, i, 0))]
    if emit_means:
        assert heads and tm % MOBA_BLOCK == 0 and tm // MOBA_BLOCK <= MEAN_ROWS
        out_shape.append(jax.ShapeDtypeStruct((m // tm, MEAN_ROWS, tn), F32))
        out_specs.append(pl.BlockSpec((1, MEAN_ROWS, tn), lambda j, i: (i, 0, 0)))
    if emit_x:
        assert heads
        out_shape.append(jax.ShapeDtypeStruct((m, k), BF16))
        out_specs.append(pl.BlockSpec((tm, k), lambda j, i: (i, 0)))
    return pl.pallas_call(
        functools.partial(_proj_kernel, heads=heads, emit_means=emit_means, emit_x=emit_x),
        out_shape=out_shape,
        grid=(n_cols, m // tm),
        in_specs=[pl.BlockSpec((tm, k), lambda j, i: (i, 0)),
                  pl.BlockSpec((k, tn), lambda j, i: (0, col_of(j)))],
        out_specs=out_specs,
        scratch_shapes=[pltpu.VMEM((k, tn), BF16)],
        compiler_params=_params("arbitrary", "arbitrary"),
        name=name,
    )(x, w)


def _moba_prompt_kernel(q_ref, k_ref, v_ref, mean_ref, *rest, seq, blk, topk, cast_blocks):
    n_src = sum(cast_blocks)
    cast_src, o_ref, cast_dst = rest[:n_src], rest[n_src], rest[n_src + 1:]
    nb = seq // blk
    hd = q_ref.shape[-1]
    q = q_ref[0]
    per_tile = nb // mean_ref.shape[0]
    means = jnp.concatenate([mean_ref[t, :per_tile, :] for t in range(mean_ref.shape[0])], axis=0)
    qs = q * (hd ** -0.5)
    kb = k_ref[...]
    vb = v_ref[...]
    key_blk = lax.broadcasted_iota(jnp.int32, (seq, hd), 0) // blk
    lane = lax.broadcasted_iota(jnp.int32, (seq, hd), 1)
    k_ext = jnp.concatenate([kb, jnp.where(key_blk == lane, 1.0, 0.0).astype(BF16)], axis=1)
    row = lax.broadcasted_iota(jnp.int32, (blk, blk), 0)
    col = lax.broadcasted_iota(jnp.int32, (blk, blk), 1)
    causal = col <= row
    blk_id = lax.broadcasted_iota(jnp.int32, (nb, blk), 0)
    for i in range(nb):
        qi = qs[i * blk:(i + 1) * blk]
        if i > topk:
            gate = lax.dot_general(means, q[i * blk:(i + 1) * blk], NT,
                                   precision=lax.Precision.HIGHEST, preferred_element_type=F32)
            beaten = jnp.zeros((nb, blk), jnp.int32)
            for m in range(i):
                gm = gate[m:m + 1, :]
                beaten = beaten + jnp.where(m < blk_id, jnp.where(gm >= gate, 1, 0), jnp.where(gm > gate, 1, 0))
            bias = jnp.where((beaten < topk) | (blk_id >= i), 0.0, NEG)
            bias = jnp.concatenate([bias, jnp.zeros((hd - nb, blk), F32)], axis=0).T
            q_ext = jnp.concatenate([qi.astype(BF16), bias.astype(BF16)], axis=1)
            s = lax.dot_general(q_ext, k_ext[:(i + 1) * blk], NT, preferred_element_type=F32)
        else:
            s = lax.dot_general(qi.astype(BF16), kb[:(i + 1) * blk], NT, preferred_element_type=F32)
        own = jnp.where(causal, s[:, i * blk:(i + 1) * blk], NEG)
        s = jnp.concatenate([s[:, :i * blk], own], axis=1) if i else own
        mx = jnp.max(s, axis=-1, keepdims=True)
        p = jnp.exp(s - mx)
        den = jnp.sum(p, axis=-1, keepdims=True)
        o = jnp.dot(p.astype(BF16), vb[:(i + 1) * blk], preferred_element_type=F32)
        o_ref[i * blk:(i + 1) * blk, :] = (o / den).astype(o_ref.dtype)
        first = 0
        for dst, n_blocks in zip(cast_dst, cast_blocks):
            parts = min(CAST_PARTS, dst.shape[0] // BF16_ROWS)
            every = nb // parts
            if i % every == every - 1:
                rows = dst.shape[0] // parts
                rsl = slice((i // every) * rows, (i // every + 1) * rows)
                for c, src in enumerate(cast_src[first:first + n_blocks]):
                    w = src.shape[1]
                    dst[rsl, c * w:(c + 1) * w] = src[rsl, :].astype(BF16)
            first += n_blocks


def _moba_prompt(p3, k2, v2, k_means, *, batch, seq, heads, hd, q_col, cast=()):
    assert heads * hd == COL_TILE
    steps = batch * heads
    tiles_per_row = k_means.shape[0] // batch
    assert k_means.shape[0] == batch * tiles_per_row and (seq // MOBA_BLOCK) % tiles_per_row == 0
    kv = pl.BlockSpec((seq, hd), lambda b, h: (b, h))
    srcs, src_specs, dst_shapes, dst_specs, cast_blocks = [], [], [], [], []
    for entry in cast:
        w, col0, cols = entry if isinstance(entry, tuple) else (entry, 0, entry.shape[1])
        rows = w.shape[0] // steps
        assert w.shape[0] % (steps * BF16_ROWS) == 0
        if col0 == 0 and cols == w.shape[1]:
            blocks = [(cols, 0)]
        else:
            assert col0 % COL_TILE == 0 and cols % COL_TILE == 0
            blocks = [(COL_TILE, col0 // COL_TILE + c) for c in range(cols // COL_TILE)]
        for width, cb in blocks:
            srcs.append(w)
            src_specs.append(pl.BlockSpec((rows, width), lambda b, h, cb=cb: (b * heads + h, cb)))
        cast_blocks.append(len(blocks))
        dst_shapes.append(jax.ShapeDtypeStruct((w.shape[0], cols), BF16))
        dst_specs.append(pl.BlockSpec((rows, cols), lambda b, h: (b * heads + h, 0)))
    kern = functools.partial(_moba_prompt_kernel, seq=seq, blk=MOBA_BLOCK, topk=MOBA_TOPK,
                             cast_blocks=tuple(cast_blocks))
    out = pl.pallas_call(
        kern,
        out_shape=[jax.ShapeDtypeStruct((batch * seq, heads * hd), BF16)] + dst_shapes,
        grid=(batch, heads),
        in_specs=[pl.BlockSpec((1, seq, hd), lambda b, h: (q_col, b, h)), kv, kv,
                  pl.BlockSpec((tiles_per_row, MEAN_ROWS, hd), lambda b, h: (b, 0, h))] + src_specs,
        out_specs=[kv] + dst_specs,
        compiler_params=_params("arbitrary", "arbitrary"),
        name="moba_prompt",
    )(p3, k2, v2, k_means, *srcs)
    return out[0], out[1:]


def _pool_prompt_kernel(u_ref, w_ref, sc_ref, o_ref, *, windows):
    seq = u_ref.shape[1]
    gc = w_ref.shape[-1]
    t = lax.broadcasted_iota(jnp.int32, (seq, 1), 0)
    for g, w in enumerate(windows):
        x = u_ref[0, :, g * gc:(g + 1) * gc]
        win = x
        width = 1
        while width < w:
            shifted = pltpu.roll(win, width, 0)
            win = win + jnp.where(t >= width, shifted, 0.0)
            width *= 2
        cnt = jnp.minimum(w, t + 1).astype(F32)
        mix = win / cnt - x
        y = jnp.dot(mix.astype(BF16), w_ref[g], preferred_element_type=F32)
        o_ref[:, g * gc:(g + 1) * gc] = (y * sc_ref[:, g * gc:(g + 1) * gc]).astype(o_ref.dtype)


def _pool_prompt(p3, w_pool, scale, *, batch, seq, u_col):
    groups, gc, _ = w_pool.shape
    width = groups * gc
    assert width == COL_TILE
    kern = functools.partial(_pool_prompt_kernel, windows=POOL_WINDOWS)
    return pl.pallas_call(
        kern,
        out_shape=jax.ShapeDtypeStruct((batch * seq, width), BF16),
        grid=(batch,),
        in_specs=[pl.BlockSpec((1, seq, width), lambda b: (u_col, b, 0)),
                  pl.BlockSpec((groups, gc, gc), lambda b: (0, 0, 0)),
                  pl.BlockSpec((1, width), lambda b: (0, 0))],
        out_specs=pl.BlockSpec((seq, width), lambda b: (b, 0)),
        compiler_params=_params("arbitrary"),
        name="pool_prompt",
    )(p3, w_pool, scale)


def _mem_prompt_kernel(q_ref, k_ref, v_ref, o_ref):
    hd = q_ref.shape[-1]
    qb = (q_ref[0] * (hd ** -0.5)).astype(BF16)
    s = lax.dot_general(qb, k_ref[0].astype(BF16), NT, preferred_element_type=F32)
    mx = jnp.max(s, axis=-1, keepdims=True)
    p = jnp.exp(s - mx)
    den = jnp.sum(p, axis=-1, keepdims=True)
    o = jnp.dot(p.astype(BF16), v_ref[0].astype(BF16), preferred_element_type=F32)
    o_ref[...] = (o / den).astype(o_ref.dtype)


def _mem_prompt(p3, kv3, *, batch, seq, n_mem, heads, hd, q_col):
    assert heads * hd == COL_TILE
    return pl.pallas_call(
        _mem_prompt_kernel,
        out_shape=jax.ShapeDtypeStruct((batch * seq, heads * hd), BF16),
        grid=(batch, heads),
        in_specs=[pl.BlockSpec((1, seq, hd), lambda b, h: (q_col, b, h)),
                  pl.BlockSpec((1, n_mem, hd), lambda b, h: (0, b, h)),
                  pl.BlockSpec((1, n_mem, hd), lambda b, h: (1, b, h))],
        out_specs=pl.BlockSpec((seq, hd), lambda b, h: (b, h)),
        compiler_params=_params("arbitrary", "arbitrary"),
        name="mem_prompt",
    )(p3, kv3, kv3)


def _merge_kernel(x_ref, a_ref, p_ref, m_ref, wa_ref, wp_ref, wm_ref, ga_ref, gp_ref, gm_ref, o_ref):
    xb = x_ref[...].astype(BF16)

    def branch(b_ref, w_ref, g_ref):
        gate = jnp.dot(xb, g_ref[...], preferred_element_type=F32)
        y = jnp.dot(b_ref[...].astype(BF16), w_ref[...], preferred_element_type=F32)
        return jax.nn.sigmoid(gate) * y
    mix = branch(a_ref, wa_ref, ga_ref) + branch(p_ref, wp_ref, gp_ref) + branch(m_ref, wm_ref, gm_ref)
    o_ref[...] = mix.astype(o_ref.dtype)


def _merge(x, attn, pool, memo, w_ba, w_bp, w_bm, w_gate, *, tm):
    m, kw = attn.shape
    d = w_ba.shape[1]
    tn = COL_TILE
    per_gate = d // tn
    row = lambda w: pl.BlockSpec((tm, w), lambda j, i: (i, 0))
    wsp = pl.BlockSpec((kw, tn), lambda j, i: (0, j))
    gsp = lambda c: pl.BlockSpec((d, tn), lambda j, i: (0, c * per_gate + j))
    return pl.pallas_call(
        _merge_kernel,
        out_shape=jax.ShapeDtypeStruct((m, d), BF16),
        grid=(d // tn, m // tm),
        in_specs=[row(d), row(kw), row(kw), row(kw), wsp, wsp, wsp, gsp(0), gsp(1), gsp(2)],
        out_specs=pl.BlockSpec((tm, tn), lambda j, i: (i, j)),
        compiler_params=_params("arbitrary", "arbitrary"),
        name="merge",
    )(x, attn, pool, memo, w_ba, w_bp, w_bm, w_gate, w_gate, w_gate)


def _oproj_kernel(x_ref, mix_ref, w_ref, g_ref, b_ref, o_ref, *, alpha):
    y = jnp.dot(mix_ref[...], w_ref[...], preferred_element_type=F32)
    o_ref[...] = _layer_norm(alpha * x_ref[...] + y, g_ref[...], b_ref[...])


def _oproj(x, mix, w_o, g, b, *, alpha, tm):
    m, d = x.shape
    row = pl.BlockSpec((tm, d), lambda i: (i, 0))
    vec = pl.BlockSpec((1, d), lambda i: (0, 0))
    return pl.pallas_call(
        functools.partial(_oproj_kernel, alpha=alpha),
        out_shape=jax.ShapeDtypeStruct((m, d), F32),
        grid=(m // tm,),
        in_specs=[row, row, pl.BlockSpec((d, d), lambda i: (0, 0)), vec, vec],
        out_specs=row,
        compiler_params=_params("arbitrary"),
        name="oproj_ln1",
    )(x, mix, w_o, g, b)


def _ffn_kernel(h_ref, wu_ref, wd_ref, g_ref, b_ref, o_ref, hb_ref, *, alpha):
    kk = pl.program_id(1)

    @pl.when(kk == 0)
    def _():
        hb_ref[...] = h_ref[...].astype(BF16)
        o_ref[...] = jnp.zeros_like(o_ref)

    a = jnp.dot(hb_ref[...], wu_ref[...], preferred_element_type=F32)
    a = jnp.square(jnp.maximum(a, 0.0)).astype(BF16)
    d = o_ref.shape[1]
    for c in range(0, d, FFN_SLAB):
        o_ref[:, c:c + FFN_SLAB] += jnp.dot(a, wd_ref[:, c:c + FFN_SLAB], preferred_element_type=F32)

    @pl.when(kk == pl.num_programs(1) - 1)
    def _():
        o_ref[...] = _layer_norm(alpha * h_ref[...] + o_ref[...], g_ref[...], b_ref[...])


def _ffn(h, w_up, w_down, g, b, *, alpha, tm, tk):
    m, d = h.shape
    dff = w_up.shape[1]
    row = pl.BlockSpec((tm, d), lambda i, k: (i, 0))
    vec = pl.BlockSpec((1, d), lambda i, k: (0, 0))
    return pl.pallas_call(
        functools.partial(_ffn_kernel, alpha=alpha),
        out_shape=jax.ShapeDtypeStruct((m, d), F32),
        grid=(m // tm, dff // tk),
        in_specs=[row,
                  pl.BlockSpec((d, tk), lambda i, k: (0, k)),
                  pl.BlockSpec((tk, d), lambda i, k: (k, 0)),
                  vec, vec],
        out_specs=row,
        scratch_shapes=[pltpu.VMEM((tm, d), BF16)],
        compiler_params=_params("arbitrary", "arbitrary"),
        name="ffn_ln2",
    )(h, w_up, w_down, g, b)


def _moba_sample(cache_k, cache_v, page_table, q, k_new, v_new, *, start_after):
    _, n_phys, page, heads, hd = cache_k.shape
    bsz, n_pages = page_table.shape
    L = V7X_SC_LANES
    topk = MOBA_TOPK
    per_blk = MOBA_BLOCK // page
    n_blk = n_pages // per_blk
    rows_per_page = page * heads
    n_ch = rows_per_page // SC_CHUNK_ROWS
    items_per_blk = per_blk * n_ch
    n_items = n_pages * n_ch
    width = heads * hd
    hv = hd // L
    groups = page // L
    n_sel_pages = topk * per_blk
    gs = n_blk + 1
    scale = hd ** -0.5
    assert hd & (hd - 1) == 0 and n_pages % L == 0 and SC_CHUNK_ROWS % heads == 0
    krows = cache_k.reshape(n_phys * rows_per_page, hd)
    vrows = cache_v.reshape(n_phys * rows_per_page, hd)
    mesh = plsc.VectorSubcoreMesh(core_axis_name="core", subcore_axis_name="subcore")
    assert bsz == mesh.num_cores * mesh.num_subcores, "one batch row per vector subcore"

    def kern(kr_hbm, vr_hbm, pt_hbm, q_hbm, kn_hbm, vn_hbm, after_hbm, o_hbm,
             buf, acc, q_v, kn_v, vn_v, pt_v, rowbuf, idx_v, s_v, out_v, gate_s, sel_s, sems, rsem):
        del after_hbm
        b = lax.axis_index("core") * mesh.num_subcores + lax.axis_index("subcore")
        pltpu.sync_copy(q_hbm.at[b], q_v)
        pltpu.sync_copy(kn_hbm.at[b], kn_v)
        pltpu.sync_copy(vn_hbm.at[b], vn_v)
        pltpu.sync_copy(pt_hbm.at[b], pt_v)
        lane = lax.iota(jnp.int32, L)

        def phys_page(pg):
            group = pt_v[pl.ds((pg // L) * L, L)]
            return jnp.sum(jnp.where(lane == pg % L, group, 0))

        def chunk_start(item, slot):
            start = phys_page(item // n_ch) * rows_per_page + (item % n_ch) * SC_CHUNK_ROWS
            pltpu.make_async_copy(kr_hbm.at[pl.ds(pl.multiple_of(start, SC_CHUNK_ROWS), SC_CHUNK_ROWS)],
                                  buf.at[slot], sems.at[slot]).start()

        def chunk_wait(slot):
            pltpu.make_async_copy(kr_hbm.at[pl.ds(0, SC_CHUNK_ROWS)], buf.at[slot], sems.at[slot]).wait()

        chunk_start(0, 0)

        @pl.loop(0, n_blk)
        def _(n):
            for h in range(heads):
                for c in range(hv):
                    acc[h, pl.ds(c * L, L)] = jnp.zeros((L,), F32)
            for j in range(items_per_blk):
                slot = j % 2
                item = n * items_per_blk + j
                chunk_wait(slot)

                @pl.when(item + 1 < n_items)
                def _():
                    chunk_start(item + 1, 1 - slot)

                @pl.loop(0, heads * hv)
                def _(hc):
                    h = hc // hv
                    cols = pl.ds((hc % hv) * L, L)
                    rows = [t * heads + h for t in range(SC_CHUNK_ROWS // heads)]
                    parts = [buf[slot, r, cols] for r in rows[:4]]
                    for i, r in enumerate(rows[4:]):
                        parts[i % 4] = parts[i % 4] + buf[slot, r, cols]
                    plsc.addupdate(acc.at[h, cols], (parts[0] + parts[1]) + (parts[2] + parts[3]))
            for h in range(heads):
                prod = acc[h, pl.ds(0, L)] * q_v[pl.ds(h * hd, L)]
                for c in range(1, hv):
                    prod = prod + acc[h, pl.ds(c * L, L)] * q_v[pl.ds(h * hd + c * L, L)]
                gate_s[h * gs + n] = jnp.sum(prod / MOBA_BLOCK)

        own_blk = n_blk
        for h in range(heads):
            open_prod = kn_v[pl.ds(h * hd, L)] * q_v[pl.ds(h * hd, L)]
            for c in range(1, hv):
                open_prod = open_prod + kn_v[pl.ds(h * hd + c * L, L)] * q_v[pl.ds(h * hd + c * L, L)]
            gate_s[h * gs + n_blk] = jnp.where(n_blk < own_blk, jnp.sum(open_prod / MOBA_BLOCK), jnp.float32(NEG))
            for j in range(topk):
                def scan(n, carry, h=h):
                    best, bi = carry
                    g = gate_s[h * gs + n]
                    better = g > best
                    return jnp.where(better, g, best), jnp.where(better, n, bi)
                _, bi = lax.fori_loop(1, gs, scan, (gate_s[h * gs], jnp.int32(0)))
                gate_s[h * gs + bi] = jnp.float32(-jnp.inf)
                sel_s[h * topk + j] = bi

        def sel_page(h, jr):
            return sel_s[h * topk + jr // per_blk] * per_blk + jr % per_blk

        def rows_copy(src_hbm, slot):
            return pltpu.make_async_copy(src_hbm.at[idx_v.at[slot]], rowbuf.at[slot], rsem.at[slot])

        def rows_start(src_hbm, h, pg, slot):
            first = phys_page(pg) * page
            for c in range(groups):
                idx_v[slot, pl.ds(c * L, L)] = (first + c * L + lane) * heads + h
            rows_copy(src_hbm, slot).start()

        @pl.loop(0, heads)
        def _(h):
            rows_start(kr_hbm, h, sel_page(h, 0), 0)
            for jr in range(n_sel_pages):
                slot = jr % 2
                rows_copy(kr_hbm, slot).wait()
                if jr + 1 < n_sel_pages:
                    rows_start(kr_hbm, h, sel_page(h, jr + 1), 1 - slot)
                else:
                    rows_start(vr_hbm, h, sel_page(h, 0), 1 - slot)

                def dots(d, accs, slot=slot):
                    col = (d + lane) & (hd - 1)
                    qd = plsc.load_gather(q_v, [h * hd + col])
                    return tuple(accs[g] + plsc.load_gather(rowbuf.at[slot], [g * L + lane, col]) * qd
                                 for g in range(groups))
                accs = lax.fori_loop(0, hd, dots, tuple(jnp.zeros((L,), F32) for _ in range(groups)))
                for g in range(groups):
                    s_v[pl.ds(jr * page + g * L, L)] = accs[g] * scale
            own = q_v[pl.ds(h * hd, L)] * kn_v[pl.ds(h * hd, L)]
            for c in range(1, hv):
                own = own + q_v[pl.ds(h * hd + c * L, L)] * kn_v[pl.ds(h * hd + c * L, L)]
            s_own = jnp.sum(own) * scale
            n_vec = n_sel_pages * groups

            def vmax(i, m):
                return jnp.maximum(m, s_v[pl.ds(i * L, L)])
            mx = jnp.maximum(jnp.max(lax.fori_loop(0, n_vec, vmax, jnp.full((L,), NEG, F32))), s_own)

            def expsum(i, tot):
                p = jnp.exp(s_v[pl.ds(i * L, L)] - mx)
                s_v[pl.ds(i * L, L)] = p
                return tot + p
            p_own = jnp.max(jnp.exp(jnp.full((L,), s_own - mx, F32)))
            den = jnp.sum(lax.fori_loop(0, n_vec, expsum, jnp.zeros((L,), F32))) + p_own
            outs = tuple(p_own * vn_v[pl.ds(h * hd + c * L, L)] for c in range(hv))
            v_first = n_sel_pages % 2
            for jr in range(n_sel_pages):
                slot = (v_first + jr) % 2
                rows_copy(vr_hbm, slot).wait()
                if jr + 1 < n_sel_pages:
                    rows_start(vr_hbm, h, sel_page(h, jr + 1), 1 - slot)

                def pv(t, o, slot=slot, jr=jr):
                    p = plsc.load_gather(s_v, [jnp.full((L,), jr * page, jnp.int32) + t])
                    return tuple(o[c] + p * rowbuf[slot, t, pl.ds(c * L, L)] for c in range(hv))
                outs = lax.fori_loop(0, page, pv, outs)
            for c in range(hv):
                out_v[pl.ds(h * hd + c * L, L)] = outs[c] / den
        pltpu.sync_copy(out_v, o_hbm.at[b])

    return pl.kernel(
        kern,
        out_type=jax.ShapeDtypeStruct((bsz, width), F32),
        mesh=mesh,
        scratch_types=[pltpu.VMEM((2, SC_CHUNK_ROWS, hd), F32),
                       pltpu.VMEM((heads, hd), F32),
                       pltpu.VMEM((width,), F32),
                       pltpu.VMEM((width,), F32),
                       pltpu.VMEM((width,), F32),
                       pltpu.VMEM((n_pages,), jnp.int32),
                       pltpu.VMEM((2, page, hd), F32),
                       pltpu.VMEM((2, page), jnp.int32),
                       pltpu.VMEM((n_sel_pages * page,), F32),
                       pltpu.VMEM((width,), F32),
                       pltpu.SMEM((heads * gs,), F32),
                       pltpu.SMEM((heads * topk,), jnp.int32),
                       pltpu.SemaphoreType.DMA((2,)),
                       pltpu.SemaphoreType.DMA((2,))],
        compiler_params=pltpu.CompilerParams(needs_layout_passes=False),
        name="moba_sample",
    )(krows, vrows, page_table, q, k_new, v_new, start_after)


def _mem_sample_kernel(q_ref, k_ref, v_ref, o_ref):
    hd = q_ref.shape[-1]
    q = q_ref[0] * (hd ** -0.5)
    s = jnp.sum(k_ref[0, 0] * q, axis=-1, keepdims=True)
    mx = jnp.max(s, axis=0, keepdims=True)
    p = jnp.exp(s - mx)
    den = jnp.sum(p, axis=0)
    o = jnp.sum(p * v_ref[0, 0], axis=0)
    o_ref[0] = o / den


def _mem_sample(q, mem_k, mem_v):
    _, bsz, n_mem, heads, hd = mem_k.shape
    tok = pl.BlockSpec((1, heads, hd), lambda b: (b, 0, 0))
    mem = pl.BlockSpec((1, 1, n_mem, heads, hd), lambda b: (0, b, 0, 0, 0))
    return pl.pallas_call(
        _mem_sample_kernel,
        out_shape=jax.ShapeDtypeStruct((bsz, heads, hd), F32),
        grid=(bsz,),
        in_specs=[tok, mem, mem],
        out_specs=tok,
        compiler_params=_params("arbitrary"),
        name="mem_sample",
    )(q, mem_k, mem_v)


def _pool_sample_kernel(u_ref, st_ref, w_ref, sc_ref, o_ref, *, windows):
    ctx = st_ref.shape[0]
    gc = w_ref.shape[-1]
    for g, w in enumerate(windows):
        sl = slice(g * gc, (g + 1) * gc)
        x = u_ref[:, sl]
        win = x
        for j in range(1, w):
            win = win + st_ref[ctx - j, :, sl]
        mix = win / float(w) - x
        y = jnp.dot(mix.astype(BF16), w_ref[g], preferred_element_type=F32)
        o_ref[:, sl] = (y * sc_ref[:, sl]).astype(o_ref.dtype)


def _pool_sample(u, state, w_pool, scale):
    bsz, width = u.shape
    return pl.pallas_call(
        functools.partial(_pool_sample_kernel, windows=POOL_WINDOWS),
        out_shape=jax.ShapeDtypeStruct((bsz, width), BF16),
        name="pool_sample",
        compiler_params=pltpu.CompilerParams(vmem_limit_bytes=V7X_VMEM_LIMIT_BYTES),
    )(u, state, w_pool, scale)


def kernel(x_prompt, x_sample, cache_k, cache_v, cache_mem_k, cache_mem_v, state_pool, page_table, mem_prompt, w_in, w_mem_kv, w_pool, pool_scale, w_br_attn, w_br_pool, w_br_mem, w_o, ln1_g, ln1_b, w_up, w_down, ln2_g, ln2_b):
    depth = w_in.shape[0]
    assert depth == 1, "single-layer trunk"
    batch, seq, d_model = x_prompt.shape
    dec_batch, dec_seq, _ = x_sample.shape
    assert dec_seq == 1
    _, n_phys, page, a_heads, a_hd = cache_k.shape
    assert page == PAGE_ROWS
    _, _, n_mem, m_heads, m_hd = cache_mem_k.shape
    pool_ctx, pool_width = state_pool.shape[2], state_pool.shape[3]
    a_width = a_heads * a_hd
    m_width = m_heads * m_hd
    alpha = float((2 * depth) ** 0.25)
    assert a_width == pool_width == m_width == COL_TILE
    k_tile, v_tile = 1, 2
    n_act_tiles = 5
    gate_cols = w_in.shape[2] - n_act_tiles * COL_TILE
    assert gate_cols == N_BRANCH * d_model

    w_pool_b = w_pool[0].astype(BF16)

    xs = x_sample.reshape(dec_batch, d_model)
    p3s = _project(xs, w_in[0], lambda j: j, n_act_tiles, tm=dec_batch, name="in_proj_sample")[0]
    q_s, k_s, v_s, u_s, mq_s = p3s[0], p3s[k_tile], p3s[v_tile], p3s[3], p3s[4]

    mp = batch * seq
    xp = x_prompt.reshape(mp, d_model)
    u_col, mq_col = 1, 2
    k_heads, k_b, k_means, xb = _project(xp, w_in[0], lambda j: k_tile, 1, tm=1024, name="k_proj_prompt",
                                         heads=a_heads, emit_means=True, emit_x=True)
    v_heads, v_b = _project(xb, w_in[0], lambda j: v_tile, 1, tm=1024, name="v_proj_prompt", heads=a_heads)
    p3 = _project(xb, w_in[0], lambda j: jnp.where(j == 0, 0, j + 2), n_act_tiles - 2,
                  tm=1024, name="in_proj_prompt")[0]
    attn_p, (w_up_b, w_down_b, w_o_b, w_ba_b, w_bp_b, w_bm_b, w_gate_b) = _moba_prompt(
        p3, k_b, v_b, k_means, batch=batch, seq=seq, heads=a_heads, hd=a_hd, q_col=0,
        cast=(w_up[0], w_down[0], w_o[0], w_br_attn[0], w_br_pool[0], w_br_mem[0],
              (w_in[0], n_act_tiles * COL_TILE, gate_cols)))
    pool_p = _pool_prompt(p3, w_pool_b, pool_scale, batch=batch, seq=seq, u_col=u_col)
    kv3 = _project(mem_prompt.reshape(batch * n_mem, d_model), w_mem_kv[0], lambda j: j, 2,
                   tm=n_mem, name="mem_kv_prompt")[0]
    memo_p = _mem_prompt(p3, kv3, batch=batch, seq=seq, n_mem=n_mem, heads=m_heads, hd=m_hd, q_col=mq_col)
    mix_p = _merge(xb, attn_p, pool_p, memo_p, w_ba_b, w_bp_b, w_bm_b, w_gate_b, tm=512)
    h_p = _oproj(xp, mix_p, w_o_b, ln1_g, ln1_b, alpha=alpha, tm=512)
    y_p = _ffn(h_p, w_up_b, w_down_b, ln2_g, ln2_b, alpha=alpha, tm=1024, tk=512)

    memo_s = _mem_sample(mq_s.reshape(dec_batch, m_heads, m_hd), cache_mem_k, cache_mem_v)
    pool_s = _pool_sample(u_s, jnp.transpose(state_pool[0], (1, 0, 2)), w_pool_b, pool_scale)
    attn_s = _moba_sample(cache_k, cache_v, page_table, q_s, k_s, v_s, start_after=k_means)
    mix_s = _merge(xs, attn_s, pool_s, memo_s.reshape(dec_batch, m_width), w_ba_b, w_bp_b, w_bm_b, w_gate_b,
                   tm=dec_batch)
    h_s = _oproj(xs, mix_s, w_o_b, ln1_g, ln1_b, alpha=alpha, tm=dec_batch)
    y_s = _ffn(h_s, w_up_b, w_down_b, ln2_g, ln2_b, alpha=alpha, tm=dec_batch, tk=2048)

    kv_shape = (depth, batch, seq, a_heads, a_hd)
    mem_shape = (depth, batch, n_mem, m_heads, m_hd)
    new_pool_p = p3.reshape(-1, batch, seq, pool_width)[u_col, :, seq - pool_ctx:, :][None]
    new_pool_s = jnp.concatenate([state_pool[0][:, 1:, :], u_s[:, None, :]], axis=1)[None]
    skv_shape = (depth, dec_batch, dec_seq, a_heads, a_hd)
    return (y_p.reshape(batch, seq, d_model), y_s.reshape(dec_batch, dec_seq, d_model),
            k_heads.reshape(kv_shape), v_heads.reshape(kv_shape),
            kv3[0].reshape(mem_shape), kv3[1].reshape(mem_shape),
            new_pool_p,
            k_s.reshape(skv_shape), v_s.reshape(skv_shape),
            new_pool_s)
```
